```python
import math
import jax, jax.numpy as jnp
from jax import lax
import numpy as np


D_MODEL = 1024
BATCH = 4
SEQ = 4096
DEPTH = 2
DEC_BATCH = 32
DEC_SEQ = 8
PAST_LEN = 8192
PAGE_SIZE = 128

HEAD_DIM = 64
H_MIX = D_MODEL // HEAD_DIM
H_SB = H_MIX // 2
H_FOX = H_MIX - H_SB
H_DIFF = D_MODEL // (2 * HEAD_DIM)
DIFF_VDIM = 2 * HEAD_DIM
W_IN_MIX = 3 * H_SB * HEAD_DIM + 3 * H_FOX * HEAD_DIM + H_FOX
W_IN_DIFF = 2 * (2 * H_DIFF * HEAD_DIM) + H_DIFF * DIFF_VDIM
N_BUCKETS = 32
MAX_DISTANCE = 128
D_FF = ((8 * D_MODEL // 3 + 127) // 128) * 128
CONV_W = 3
Q_BLOCK = 128
N_EVEN = (DEPTH + 1) // 2
N_ODD = DEPTH // 2
EPS = 1e-6

kernel_name = 'stickbreak_fox_diffattn_convffn_step'


def rmsnorm(x, g):
    xf = x.astype(jnp.float32)
    y = xf * lax.rsqrt(jnp.mean(xf * xf, axis=-1, keepdims=True) + EPS)
    return (y * g.astype(jnp.float32)).astype(x.dtype)


def gather_pages(cache, layer, page_table):
    rows = cache[layer, page_table]
    db, npg = page_table.shape
    return rows.reshape((db, npg * rows.shape[2]) + rows.shape[3:])


def mix0_project(h, w_in, b_f, qn, kn):
    B, T, _ = h.shape
    proj = h @ w_in
    ws = H_SB * HEAD_DIM
    wf = H_FOX * HEAD_DIM
    cuts = [ws, 2 * ws, 3 * ws, 3 * ws + wf, 3 * ws + 2 * wf, 3 * ws + 3 * wf]
    q_sb, k_sb, v_sb, q_fx, k_fx, v_fx, f_logit = jnp.split(proj, cuts, axis=-1)
    q_sb = q_sb.reshape(B, T, H_SB, HEAD_DIM)
    k_sb = k_sb.reshape(B, T, H_SB, HEAD_DIM)
    v_sb = v_sb.reshape(B, T, H_SB, HEAD_DIM)
    q_fx = rmsnorm(q_fx.reshape(B, T, H_FOX, HEAD_DIM), qn)
    k_fx = rmsnorm(k_fx.reshape(B, T, H_FOX, HEAD_DIM), kn)
    v_fx = v_fx.reshape(B, T, H_FOX, HEAD_DIM)
    logf = jax.nn.log_sigmoid(f_logit.astype(jnp.float32) + b_f.astype(jnp.float32))
    k_all = jnp.concatenate([k_sb, k_fx], axis=2)
    v_all = jnp.concatenate([v_sb, v_fx], axis=2)
    return q_sb, q_fx, k_all, v_all, logf


def mix0_attend(q_sb, q_fx, k_all, v_all, c_q, c_k, q_pos, k_pos):
    B, Tq = q_sb.shape[:2]
    scale = HEAD_DIM ** -0.5
    k_sb, k_fx = k_all[:, :, :H_SB], k_all[:, :, H_SB:]
    v_sb, v_fx = v_all[:, :, :H_SB], v_all[:, :, H_SB:]
    z = jnp.einsum('bqhd,bkhd->bhqk', q_sb, k_sb).astype(jnp.float32) * scale
    strict = k_pos[None, :] < q_pos[:, None]
    log_1m = jnp.where(strict, jax.nn.log_sigmoid(-z), 0.0)
    between = lax.cumsum(log_1m, axis=3, reverse=True) - log_1m
    a_sb = jnp.where(strict, jnp.exp(jax.nn.log_sigmoid(z) + between), 0.0)
    o_sb = jnp.einsum('bhqk,bkhd->bqhd', a_sb.astype(v_sb.dtype), v_sb)
    s = jnp.einsum('bqhd,bkhd->bhqk', q_fx, k_fx).astype(jnp.float32) * scale
    s = s + jnp.transpose(c_q, (0, 2, 1))[..., None] - jnp.transpose(c_k, (0, 2, 1))[:, :, None, :]
    causal = k_pos[None, :] <= q_pos[:, None]
    p = jax.nn.softmax(jnp.where(causal, s, -jnp.inf), axis=-1)
    o_fx = jnp.einsum('bhqk,bkhd->bqhd', p.astype(v_fx.dtype), v_fx)
    o = jnp.concatenate([o_sb, o_fx], axis=2)
    return o.reshape(B, Tq, H_MIX * HEAD_DIM)


def mix0_prompt(h, w_in, b_f, qn, kn, w_out):
    T = h.shape[1]
    q_sb, q_fx, k_all, v_all, logf = mix0_project(h, w_in, b_f, qn, kn)
    c = jnp.cumsum(logf, axis=1)
    pos = jnp.arange(T, dtype=jnp.int32)
    outs = []
    for start in range(0, T, Q_BLOCK):
        end = min(start + Q_BLOCK, T)
        outs.append(mix0_attend(q_sb[:, start:end], q_fx[:, start:end], k_all[:, :end], v_all[:, :end],
                                c[:, start:end], c[:, :end], pos[start:end], pos[:end]))
    o = jnp.concatenate(outs, axis=1)
    return o @ w_out, k_all, v_all, logf


def mix0_sample(h, cache_k, cache_v, cache_logf, layer, page_table, w_in, b_f, qn, kn, w_out):
    T = h.shape[1]
    past = page_table.shape[1] * PAGE_SIZE
    q_sb, q_fx, k_new, v_new, logf_new = mix0_project(h, w_in, b_f, qn, kn)
    k_all = jnp.concatenate([gather_pages(cache_k, layer, page_table), k_new], axis=1)
    v_all = jnp.concatenate([gather_pages(cache_v, layer, page_table), v_new], axis=1)
    logf_all = jnp.concatenate([gather_pages(cache_logf, layer, page_table).astype(jnp.float32), logf_new], axis=1)
    c = jnp.cumsum(logf_all, axis=1)
    q_pos = past + jnp.arange(T, dtype=jnp.int32)
    k_pos = jnp.arange(past + T, dtype=jnp.int32)
    o = mix0_attend(q_sb, q_fx, k_all, v_all, c[:, past:], c, q_pos, k_pos)
    return o @ w_out, k_new, v_new, logf_new


def t5_bias(q_pos, k_pos, rel_bias):
    rel = jnp.maximum(q_pos[:, None] - k_pos[None, :], 0)
    max_exact = N_BUCKETS // 2
    relf = jnp.maximum(rel, 1).astype(jnp.float32)
    large = max_exact + (jnp.log(relf / max_exact) / math.log(MAX_DISTANCE / max_exact)
                         * (N_BUCKETS - max_exact)).astype(jnp.int32)
    large = jnp.minimum(large, N_BUCKETS - 1)
    bucket = jnp.where(rel < max_exact, rel, large)
    return jnp.transpose(rel_bias[bucket], (2, 0, 1)).astype(jnp.float32)


def diff_lambda(lq1, lk1, lq2, lk2, lam_init):
    f = jnp.float32
    return (jnp.exp(jnp.sum(lq1.astype(f) * lk1.astype(f)))
            - jnp.exp(jnp.sum(lq2.astype(f) * lk2.astype(f))) + lam_init)


def diff_project(h, w_in, qn, kn):
    B, T, _ = h.shape
    proj = h @ w_in
    wq = 2 * H_DIFF * HEAD_DIM
    q, k, v = jnp.split(proj, [wq, 2 * wq], axis=-1)
    q = rmsnorm(q.reshape(B, T, H_DIFF, 2, HEAD_DIM), qn)
    k = rmsnorm(k.reshape(B, T, H_DIFF, 2, HEAD_DIM), kn)
    v = v.reshape(B, T, H_DIFF, DIFF_VDIM)
    return q, k, v


def diff_attend(q, k, v, bias, lam, lam_init, subln, q_pos, k_pos):
    B, Tq = q.shape[:2]
    s = jnp.einsum('bqhmd,bkhmd->bmhqk', q, k).astype(jnp.float32) * (HEAD_DIM ** -0.5) + bias[None, None]
    causal = k_pos[None, :] <= q_pos[:, None]
    p = jax.nn.softmax(jnp.where(causal, s, -jnp.inf), axis=-1)
    w = p[:, 0] - lam * p[:, 1]
    o = jnp.einsum('bhqk,bkhe->bqhe', w.astype(v.dtype), v)
    o = rmsnorm(o, subln) * (1.0 - lam_init)
    return o.reshape(B, Tq, H_DIFF * DIFF_VDIM)


def diff_prompt(h, w_in, qn, kn, rel_bias, lam, lam_init, subln, w_out):
    T = h.shape[1]
    q, k, v = diff_project(h, w_in, qn, kn)
    pos = jnp.arange(T, dtype=jnp.int32)
    outs = []
    for start in range(0, T, Q_BLOCK):
        end = min(start + Q_BLOCK, T)
        bias = t5_bias(pos[start:end], pos[:end], rel_bias)
        outs.append(diff_attend(q[:, start:end], k[:, :end], v[:, :end], bias, lam, lam_init, subln,
                                pos[start:end], pos[:end]))
    o = jnp.concatenate(outs, axis=1)
    return o @ w_out, k, v


def diff_sample(h, cache_k, cache_v, layer, page_table, w_in, qn, kn, rel_bias, lam, lam_init, subln, w_out):
    T = h.shape[1]
    past = page_table.shape[1] * PAGE_SIZE
    q, k_new, v_new = diff_project(h, w_in, qn, kn)
    k_all = jnp.concatenate([gather_pages(cache_k, layer, page_table), k_new], axis=1)
    v_all = jnp.concatenate([gather_pages(cache_v, layer, page_table), v_new], axis=1)
    q_pos = past + jnp.arange(T, dtype=jnp.int32)
    k_pos = jnp.arange(past + T, dtype=jnp.int32)
    bias = t5_bias(q_pos, k_pos, rel_bias)
    o = diff_attend(q, k_all, v_all, bias, lam, lam_init, subln, q_pos, k_pos)
    return o @ w_out, k_new, v_new


def conv_ffn(h, prev, w_gate, w_up, conv_w, conv_b, w_down):
    T = h.shape[1]
    g = h @ w_gate
    u = h @ w_up
    gx = jnp.concatenate([prev.astype(g.dtype), g], axis=1)
    gc = conv_b
    for i in range(CONV_W):
        gc = gc + conv_w[i] * gx[:, i:i + T]
    y = (jax.nn.silu(gc) * u) @ w_down
    return y, gx[:, -(CONV_W - 1):]


def setup_inputs(seed: int = 0) -> dict:
    key = jax.random.key(seed)
    ks = jax.random.split(key, 40)
    f32 = jnp.float32
    n_pages = PAST_LEN // PAGE_SIZE
    n_used = DEC_BATCH * n_pages
    n_pool = n_used + max(1, n_used // 4)

    def nrm(k, shape, s=1.0):
        return jax.random.normal(k, shape, f32) * s

    def gain(k, shape):
        return 1.0 + 0.01 * jax.random.normal(k, shape, f32)

    page_table = jax.random.permutation(ks[0], n_pool)[:n_used].reshape(DEC_BATCH, n_pages).astype(jnp.int32)
    return {
        'x_prompt': nrm(ks[1], (BATCH, SEQ, D_MODEL)),
        'x_sample': nrm(ks[2], (DEC_BATCH, DEC_SEQ, D_MODEL)),
        'cache_k_mix': nrm(ks[3], (N_EVEN, n_pool, PAGE_SIZE, H_MIX, HEAD_DIM)),
        'cache_v_mix': nrm(ks[4], (N_EVEN, n_pool, PAGE_SIZE, H_MIX, HEAD_DIM)),
        'cache_logf_mix': jax.nn.log_sigmoid(2.5 + nrm(ks[5], (N_EVEN, n_pool, PAGE_SIZE, H_FOX))),
        'cache_k_diff': nrm(ks[6], (N_ODD, n_pool, PAGE_SIZE, H_DIFF, 2, HEAD_DIM)),
        'cache_v_diff': nrm(ks[7], (N_ODD, n_pool, PAGE_SIZE, H_DIFF, DIFF_VDIM)),
        'state_ffn_conv': nrm(ks[8], (DEPTH, DEC_BATCH, CONV_W - 1, D_FF)),
        'page_table': page_table,
        'rel_bias': nrm(ks[9], (N_BUCKETS, H_DIFF), 0.5),
        'norm_mix': gain(ks[10], (N_EVEN, D_MODEL)),
        'w_in_mix': nrm(ks[11], (N_EVEN, D_MODEL, W_IN_MIX), D_MODEL ** -0.5),
        'b_forget': jax.random.uniform(ks[12], (N_EVEN, H_FOX), f32, 1.0, 4.0),
        'qnorm_fox': gain(ks[13], (N_EVEN, HEAD_DIM)),
        'knorm_fox': gain(ks[14], (N_EVEN, HEAD_DIM)),
        'w_out_mix': nrm(ks[15], (N_EVEN, H_MIX * HEAD_DIM, D_MODEL), (H_MIX * HEAD_DIM) ** -0.5),
        'norm_diff': gain(ks[16], (N_ODD, D_MODEL)),
        'w_in_diff': nrm(ks[17], (N_ODD, D_MODEL, W_IN_DIFF), D_MODEL ** -0.5),
        'qnorm_diff': gain(ks[18], (N_ODD, HEAD_DIM)),
        'knorm_diff': gain(ks[19], (N_ODD, HEAD_DIM)),
        'lambda_q1': nrm(ks[20], (N_ODD, HEAD_DIM), 0.1),
        'lambda_k1': nrm(ks[21], (N_ODD, HEAD_DIM), 0.1),
        'lambda_q2': nrm(ks[22], (N_ODD, HEAD_DIM), 0.1),
        'lambda_k2': nrm(ks[23], (N_ODD, HEAD_DIM), 0.1),
        'subln_diff': gain(ks[24], (N_ODD, DIFF_VDIM)),
        'w_out_diff': nrm(ks[25], (N_ODD, H_DIFF * DIFF_VDIM, D_MODEL), (H_DIFF * DIFF_VDIM) ** -0.5),
        'norm_ffn': gain(ks[26], (DEPTH, D_MODEL)),
        'w_gate': nrm(ks[27], (DEPTH, D_MODEL, D_FF), D_MODEL ** -0.5),
        'w_up': nrm(ks[28], (DEPTH, D_MODEL, D_FF), D_MODEL ** -0.5),
        'conv_w': nrm(ks[29], (DEPTH, CONV_W, D_FF), CONV_W ** -0.5),
        'conv_b': nrm(ks[30], (DEPTH, D_FF), 0.01),
        'w_down': nrm(ks[31], (DEPTH, D_FF, D_MODEL), D_FF ** -0.5),
    }


def reference(x_prompt, x_sample, cache_k_mix, cache_v_mix, cache_logf_mix, cache_k_diff, cache_v_diff,
              state_ffn_conv, page_table, rel_bias, norm_mix, w_in_mix, b_forget, qnorm_fox, knorm_fox,
              w_out_mix, norm_diff, w_in_diff, qnorm_diff, knorm_diff, lambda_q1, lambda_k1, lambda_q2,
              lambda_k2, subln_diff, w_out_diff, norm_ffn, w_gate, w_up, conv_w, conv_b, w_down):
    xp, xs = x_prompt, x_sample
    kmix_p, vmix_p, logf_p, kmix_s, vmix_s, logf_s = [], [], [], [], [], []
    kdiff_p, vdiff_p, kdiff_s, vdiff_s = [], [], [], []
    conv_p, conv_s = [], []
    for l in range(DEPTH):
        j = l // 2
        if l % 2 == 0:
            hp = rmsnorm(xp, norm_mix[j])
            hs = rmsnorm(xs, norm_mix[j])
            yp, kp, vp, fp = mix0_prompt(hp, w_in_mix[j], b_forget[j], qnorm_fox[j], knorm_fox[j], w_out_mix[j])
            ys, kq, vq, fq = mix0_sample(hs, cache_k_mix, cache_v_mix, cache_logf_mix, j, page_table,
                                         w_in_mix[j], b_forget[j], qnorm_fox[j], knorm_fox[j], w_out_mix[j])
            kmix_p.append(kp); vmix_p.append(vp); logf_p.append(fp)
            kmix_s.append(kq); vmix_s.append(vq); logf_s.append(fq)
        else:
            lam_init = 0.8 - 0.6 * math.exp(-0.3 * l)
            lam = diff_lambda(lambda_q1[j], lambda_k1[j], lambda_q2[j], lambda_k2[j], lam_init)
            hp = rmsnorm(xp, norm_diff[j])
            hs = rmsnorm(xs, norm_diff[j])
            yp, kp, vp = diff_prompt(hp, w_in_diff[j], qnorm_diff[j], knorm_diff[j], rel_bias, lam, lam_init,
                                     subln_diff[j], w_out_diff[j])
            ys, kq, vq = diff_sample(hs, cache_k_diff, cache_v_diff, j, page_table, w_in_diff[j], qnorm_diff[j],
                                     knorm_diff[j], rel_bias, lam, lam_init, subln_diff[j], w_out_diff[j])
            kdiff_p.append(kp); vdiff_p.append(vp)
            kdiff_s.append(kq); vdiff_s.append(vq)
        xp = xp + yp
        xs = xs + ys
        hp = rmsnorm(xp, norm_ffn[l])
        hs = rmsnorm(xs, norm_ffn[l])
        zeros_prev = jnp.zeros((hp.shape[0], CONV_W - 1, D_FF), hp.dtype)
        fyp, cp = conv_ffn(hp, zeros_prev, w_gate[l], w_up[l], conv_w[l], conv_b[l], w_down[l])
        fys, cs = conv_ffn(hs, state_ffn_conv[l], w_gate[l], w_up[l], conv_w[l], conv_b[l], w_down[l])
        xp = xp + fyp
        xs = xs + fys
        conv_p.append(cp); conv_s.append(cs)
    return (xp, xs, jnp.stack(kmix_p), jnp.stack(vmix_p), jnp.stack(logf_p), jnp.stack(kmix_s),
            jnp.stack(vmix_s), jnp.stack(logf_s), jnp.stack(kdiff_p), jnp.stack(vdiff_p), jnp.stack(kdiff_s),
            jnp.stack(vdiff_s), jnp.stack(conv_p), jnp.stack(conv_s))
```

```python
import functools
import math

import numpy as np
import jax
import jax.numpy as jnp
from jax import lax
from jax.experimental import pallas as pl
from jax.experimental.pallas import tpu as pltpu

F32 = jnp.float32
BF16 = jnp.bfloat16

HEAD_DIM = 64
N_BUCKETS = 32
MAX_DISTANCE = 128
CONV_W = 3
EPS = 1e-6
QK_SCALE = HEAD_DIM ** -0.5

LANES = 128
SUBLANES = 8
VMEM_LIMIT = 56 * 1024 * 1024

BLK = LANES
NEG = -1e30
EXP_ZERO_BELOW = -104.0


def _cparams(*sem):
    return pltpu.CompilerParams(dimension_semantics=sem, vmem_limit_bytes=VMEM_LIMIT)


def _dot(a, b):
    return jnp.dot(a, b, preferred_element_type=F32)


def _dot_nt(a, b):
    return lax.dot_general(a, b, (((1,), (1,)), ((), ())), preferred_element_type=F32)


def _dot_f32(a, b):
    return jnp.dot(a, b, preferred_element_type=F32, precision=lax.Precision.HIGHEST)


def _split_dot(a, b):
    hi = a.astype(BF16)
    lo = (a - hi.astype(F32)).astype(BF16)
    return _dot(hi, b) + _dot(lo, b)


def _log_sigmoid(x):
    return jnp.minimum(x, 0.0) - jnp.log1p(jnp.exp(-jnp.abs(x)))


def _rmsnorm(x, g):
    return x * lax.rsqrt(jnp.mean(x * x, axis=-1, keepdims=True) + EPS) * g


def _head_rmsnorm(y, gmat2, gain):
    outs = []
    for c in range(y.shape[1] // LANES):
        yc = y[:, c * LANES:(c + 1) * LANES]
        sq = yc * yc
        hi = sq.astype(BF16)
        lo = (sq - hi.astype(F32)).astype(BF16)
        ms = _dot(jnp.concatenate([hi, lo], axis=1), gmat2)
        outs.append(yc * lax.rsqrt(ms + EPS))
    return jnp.concatenate(outs, axis=1) * gain


def _head_mean_matrix():
    idx = np.arange(LANES) // HEAD_DIM
    g = (idx[:, None] == idx[None, :]).astype(np.float32) / HEAD_DIM
    return jnp.asarray(np.concatenate([g, g], axis=0), BF16)


def _head0_mask(dtype):
    lane = lax.broadcasted_iota(jnp.int32, (BLK, BLK), 1)
    if dtype == F32:
        return lane < HEAD_DIM
    return jnp.where(lane < HEAD_DIM, 1.0, 0.0).astype(dtype) > 0


def _block_diag_pair(x, lo_half):
    zero = jnp.zeros_like(x)
    return jnp.concatenate([jnp.where(lo_half, x, zero), jnp.where(lo_half, zero, x)], axis=0)


def _proj_mix_kernel(with_cumsum, x_ref, g_ref, w_ref, wf_ref, wft_ref, bfr_ref, bfc_ref, qn_ref, kn_ref,
                     gmat_ref, q_ref, k_ref, v_ref, kb_ref, vb_ref, logf_ref, *rest):
    ws = q_ref.shape[2] // 2
    n_fox = logf_ref.shape[2]
    tm = x_ref.shape[1]
    hb = _rmsnorm(x_ref[0], g_ref[...]).astype(BF16)

    def sec(i):
        return _dot(hb, w_ref[:, i * ws:(i + 1) * ws])

    q_ref[0, :, :ws] = (sec(0) * QK_SCALE).astype(BF16)
    k_sb = sec(1)
    k_ref[0, :, :ws] = k_sb
    kb_ref[0, :, :ws] = k_sb.astype(BF16)
    v_sb = sec(2)
    v_ref[0, :, :ws] = v_sb
    vb_ref[0, :, :ws] = v_sb.astype(BF16)
    gmat = gmat_ref[...]
    q_ref[0, :, ws:] = (_head_rmsnorm(sec(3), gmat, qn_ref[...]) * QK_SCALE).astype(BF16)
    k_fx = _head_rmsnorm(sec(4), gmat, kn_ref[...])
    k_ref[0, :, ws:] = k_fx
    kb_ref[0, :, ws:] = k_fx.astype(BF16)
    v_fx = sec(5)
    v_ref[0, :, ws:] = v_fx
    vb_ref[0, :, ws:] = v_fx.astype(BF16)

    logf = _log_sigmoid(_dot(hb, wf_ref[...]) + bfr_ref[...])
    logf_ref[0] = logf[:, :n_fox]
    if not with_cumsum:
        return
    c_ref, ct_ref, carry_ref, carryt_ref = rest

    @pl.when(pl.program_id(1) == 0)
    def _():
        carry_ref[...] = jnp.zeros_like(carry_ref)
        carryt_ref[...] = jnp.zeros_like(carryt_ref)

    r = lax.broadcasted_iota(jnp.int32, (tm, tm), 0)
    c = lax.broadcasted_iota(jnp.int32, (tm, tm), 1)
    csum = _dot_f32(jnp.where(c <= r, 1.0, 0.0).astype(F32), logf) + carry_ref[...]
    c_ref[0] = csum[:, :n_fox]
    carry_ref[...] = csum[tm - 1:tm, :]
    logft = _log_sigmoid(_dot_nt(wft_ref[...], hb) + bfc_ref[:, 0:1])
    csumt = _dot_f32(logft, jnp.where(r <= c, 1.0, 0.0).astype(F32)) + carryt_ref[:, 0:1]
    ct_ref[0] = csumt[:n_fox, :]
    carryt_ref[...] = jnp.broadcast_to(csumt[:, tm - 1:tm], carryt_ref.shape)


def _proj_mix(x, g, w_in, b_f, qn, kn, *, tm, with_cumsum):
    B, T, D = x.shape
    n_fox = b_f.shape[0]
    ws = (w_in.shape[1] - n_fox) // 6
    wmain = w_in[:, :6 * ws].astype(BF16)
    wf = jnp.zeros((D, LANES), BF16).at[:, :n_fox].set(w_in[:, 6 * ws:].astype(BF16))
    wft = jnp.zeros((2 * SUBLANES, D), BF16).at[:n_fox, :].set(w_in[:, 6 * ws:].T.astype(BF16))
    bfr = jnp.zeros((1, LANES), F32).at[0, :n_fox].set(b_f)
    bfc = jnp.zeros((2 * SUBLANES, LANES), F32).at[:n_fox, :].set(jnp.broadcast_to(b_f[:, None], (n_fox, LANES)))
    reps = ws // HEAD_DIM
    full = lambda a: pl.BlockSpec(a.shape, lambda b, t: (0,) * a.ndim)
    row = lambda w: pl.BlockSpec((1, tm, w), lambda b, t: (b, t, 0))
    args = (x, g.reshape(1, D), wmain, wf, wft, bfr, bfc, jnp.tile(qn, reps).reshape(1, ws),
            jnp.tile(kn, reps).reshape(1, ws), _head_mean_matrix())
    out_shape = [jax.ShapeDtypeStruct((B, T, 2 * ws), BF16), jax.ShapeDtypeStruct((B, T, 2 * ws), F32),
                 jax.ShapeDtypeStruct((B, T, 2 * ws), F32), jax.ShapeDtypeStruct((B, T, 2 * ws), BF16),
                 jax.ShapeDtypeStruct((B, T, 2 * ws), BF16), jax.ShapeDtypeStruct((B, T, n_fox), F32)]
    out_specs = [row(2 * ws)] * 5 + [row(n_fox)]
    scratch = []
    if with_cumsum:
        out_shape += [jax.ShapeDtypeStruct((B, T, n_fox), F32), jax.ShapeDtypeStruct((B, n_fox, T), F32)]
        out_specs += [row(n_fox), pl.BlockSpec((1, n_fox, tm), lambda b, t: (b, 0, t))]
        scratch = [pltpu.VMEM((1, LANES), F32), pltpu.VMEM((2 * SUBLANES, LANES), F32)]
    return pl.pallas_call(
        functools.partial(_proj_mix_kernel, with_cumsum),
        grid=(B, T // tm),
        in_specs=[row(D)] + [full(a) for a in args[1:]],
        out_specs=out_specs, out_shape=out_shape, scratch_shapes=scratch,
        compiler_params=_cparams("parallel", "arbitrary"), name="proj_mix",
    )(*args)


def _sb_prompt_kernel(q_ref, k_ref, v_ref, o_ref, r_ref, acc_ref):
    T = q_ref.shape[1]
    lane = lax.broadcasted_iota(jnp.int32, (BLK, BLK), 1)
    row = lax.broadcasted_iota(jnp.int32, (BLK, BLK), 0)
    lo_half = _head0_mask(BF16)
    strict = lane < row
    strict2 = jnp.concatenate([strict, strict], axis=1)
    r2 = lax.broadcasted_iota(jnp.int32, (2 * BLK, 2 * BLK), 0) & (BLK - 1)
    c2 = lax.broadcasted_iota(jnp.int32, (2 * BLK, 2 * BLK), 1)
    suffix = jnp.where((c2 >= BLK) | (r2 > c2), 1.0, 0.0).astype(BF16)

    def tile(q2, j, masked):
        ks = pl.ds(pl.multiple_of(j * BLK, BLK), BLK)
        z = _dot_nt(q2, _block_diag_pair(k_ref[0, ks, :], lo_half))
        ls = _log_sigmoid(z)
        l1m = ls - z
        if masked:
            l1m = jnp.where(strict2, l1m, 0.0)
        a = []
        for h in range(2):
            hs = slice(h * BLK, (h + 1) * BLK)
            lh = l1m[:, hs]
            hi = lh.astype(BF16)
            lo = (lh - hi.astype(F32)).astype(BF16)
            cs = _dot(jnp.concatenate([hi, lo], axis=1), suffix)
            run = r_ref[:, hs]
            ah = jnp.exp(ls[:, hs] + cs[:, :BLK] + run)
            if masked:
                ah = jnp.where(strict, ah, 0.0)
            a.append(ah.astype(BF16))
            r_ref[:, hs] = run + cs[:, BLK:]
        acc_ref[...] += _dot(jnp.concatenate(a, axis=1), _block_diag_pair(v_ref[0, ks, :], lo_half))

    def qblock(i, carry):
        qs = pl.ds(pl.multiple_of(i * BLK, BLK), BLK)
        q2 = q_ref[0, qs, :]
        r_ref[...] = jnp.zeros_like(r_ref)
        acc_ref[...] = jnp.zeros_like(acc_ref)
        tile(q2, i, True)

        def alive():
            return (jnp.max(r_ref[...]) > EXP_ZERO_BELOW).astype(jnp.int32)

        def cond(s):
            return (s[0] >= 0) & (s[1] > 0)

        def body(s):
            tile(q2, s[0], False)
            return s[0] - 1, alive()

        lax.while_loop(cond, body, (i - 1, alive()))
        o_ref[0, qs, :] = acc_ref[...].astype(o_ref.dtype)
        return carry

    lax.fori_loop(0, T // BLK, qblock, 0)


def _sb_prompt(q, kb, vb, n_pairs):
    B, T, _ = q.shape
    spec = pl.BlockSpec((1, T, LANES), lambda b, p: (b, 0, p))
    return pl.pallas_call(
        _sb_prompt_kernel, grid=(B, n_pairs), in_specs=[spec] * 3, out_specs=spec,
        out_shape=jax.ShapeDtypeStruct((B, T, n_pairs * LANES), BF16),
        scratch_shapes=[pltpu.VMEM((BLK, 2 * BLK), F32), pltpu.VMEM((BLK, BLK), F32)],
        compiler_params=_cparams("parallel", "parallel"), name="sb_prompt",
    )(q, kb, vb)


def _fox_prompt_kernel(q_ref, k_ref, v_ref, c_ref, ct_ref, o_ref, m_ref, acc_ref):
    T = q_ref.shape[1]
    lane = lax.broadcasted_iota(jnp.int32, (BLK, BLK), 1)
    row = lax.broadcasted_iota(jnp.int32, (BLK, BLK), 0)
    lo_half = _head0_mask(BF16)
    causal = lane <= row
    causal2 = jnp.concatenate([causal, causal], axis=1)
    one = jnp.ones((BLK, BLK), BF16)
    zero = jnp.zeros((BLK, BLK), BF16)
    ones_lo = jnp.where(lo_half, one, zero)
    ones_hi = jnp.where(lo_half, zero, one)

    def tile(q2, cq2, j, masked):
        ks = pl.ds(pl.multiple_of(j * BLK, BLK), BLK)
        z = _dot_nt(q2, _block_diag_pair(k_ref[0, ks, :], lo_half))
        ck2 = jnp.concatenate([ct_ref[0, 0, 0:1, ks], ct_ref[0, 0, 1:2, ks]], axis=1)
        s = z + cq2 - ck2
        if masked:
            s = jnp.where(causal2, s, NEG)
        mn0 = jnp.maximum(m_ref[:, 0:1], jnp.max(s[:, :BLK], axis=1, keepdims=True))
        mn1 = jnp.maximum(m_ref[:, HEAD_DIM:HEAD_DIM + 1], jnp.max(s[:, BLK:], axis=1, keepdims=True))
        p = jnp.exp(s - jnp.concatenate([jnp.broadcast_to(mn0, (BLK, BLK)), jnp.broadcast_to(mn1, (BLK, BLK))], axis=1))
        m_new = jnp.where(_head0_mask(F32), mn0, mn1)
        alpha = jnp.exp(m_ref[...] - m_new)
        m_ref[...] = m_new
        vb = v_ref[0, ks, :]
        vaug = jnp.concatenate([
            jnp.concatenate([jnp.where(lo_half, vb, zero), ones_lo], axis=1),
            jnp.concatenate([jnp.where(lo_half, zero, vb), ones_hi], axis=1)], axis=0)
        acc_ref[...] = jnp.concatenate([alpha, alpha], axis=1) * acc_ref[...] + _dot(p.astype(BF16), vaug)

    def qblock(i, carry):
        qs = pl.ds(pl.multiple_of(i * BLK, BLK), BLK)
        q2 = q_ref[0, qs, :]
        cq2 = jnp.concatenate([jnp.broadcast_to(c_ref[0, 0, qs, 0:1], (BLK, BLK)),
                               jnp.broadcast_to(c_ref[0, 0, qs, 1:2], (BLK, BLK))], axis=1)
        m_ref[...] = jnp.full_like(m_ref, NEG)
        acc_ref[...] = jnp.zeros_like(acc_ref)
        tile(q2, cq2, i, True)
        lax.fori_loop(0, i, lambda j, c: (tile(q2, cq2, j, False), c)[1], 0)
        o_ref[0, qs, :] = (acc_ref[:, :BLK] / acc_ref[:, BLK:]).astype(o_ref.dtype)
        return carry

    lax.fori_loop(0, T // BLK, qblock, 0)


def _fox_prompt(q, kb, vb, c, ct, first_pair, n_pairs):
    B, T, _ = q.shape
    spec = pl.BlockSpec((1, T, LANES), lambda b, p: (b, 0, first_pair + p))
    c4 = c.reshape(B, T, n_pairs, 2).transpose(0, 2, 1, 3)
    ct4 = ct.reshape(B, n_pairs, 2, T)
    return pl.pallas_call(
        _fox_prompt_kernel, grid=(B, n_pairs),
        in_specs=[spec] * 3 + [pl.BlockSpec((1, 1, T, 2), lambda b, p: (b, p, 0, 0)),
                               pl.BlockSpec((1, 1, 2, T), lambda b, p: (b, p, 0, 0))],
        out_specs=pl.BlockSpec((1, T, LANES), lambda b, p: (b, 0, p)),
        out_shape=jax.ShapeDtypeStruct((B, T, n_pairs * LANES), BF16),
        scratch_shapes=[pltpu.VMEM((BLK, BLK), F32), pltpu.VMEM((BLK, 2 * BLK), F32)],
        compiler_params=_cparams("parallel", "parallel"), name="fox_prompt",
    )(q, kb, vb, c4, ct4)


def _out_proj_kernel(n_parts, x_ref, *refs):
    o_ref = refs[-1]
    y = x_ref[...]
    for i in range(n_parts):
        y = y + _dot(refs[i][...], refs[n_parts + i][...])
    o_ref[...] = y


def _out_proj(x2, parts, w_out, *, tm):
    M, D = x2.shape
    ws, off = [], 0
    for p in parts:
        ws.append(w_out[off:off + p.shape[1]].astype(BF16))
        off += p.shape[1]
    n = len(parts)
    return pl.pallas_call(
        functools.partial(_out_proj_kernel, n), grid=(M // tm,),
        in_specs=[pl.BlockSpec((tm, D), lambda i: (i, 0))]
                 + [pl.BlockSpec((tm, p.shape[1]), lambda i: (i, 0)) for p in parts]
                 + [pl.BlockSpec(w.shape, lambda i: (0, 0)) for w in ws],
        out_specs=pl.BlockSpec((tm, D), lambda i: (i, 0)),
        out_shape=jax.ShapeDtypeStruct((M, D), F32),
        compiler_params=_cparams("parallel"), name="out_proj",
    )(x2, *parts, *ws)


FF_CHUNK = 2 * LANES
HALO = 2 * SUBLANES


def _silu(x):
    return x * (1.0 / (1.0 + jnp.exp(-x)))


def _ffn_chunk(cs, hcat, hb, g_prev, wg_ref, wu_ref, cw_ref, cb_ref, gs_ref, act_ref):
    tm = hb.shape[0]
    gx = _dot(hcat, wg_ref[:, cs])
    gs_ref[...] = gx
    g0 = gx[gx.shape[0] - tm:]
    g1 = g_prev(1, gs_ref[HALO - 1:HALO - 1 + tm, :])
    g2 = g_prev(2, gs_ref[HALO - 2:HALO - 2 + tm, :])
    gc = cb_ref[:, cs] + cw_ref[0:1, cs] * g2
    gc = gc + cw_ref[1:2, cs] * g1
    gc = gc + cw_ref[2:3, cs] * g0
    act_ref[:, cs] = (_silu(gc) * _dot(hb, wu_ref[:, cs])).astype(BF16)
    return g0


def _ffn_prompt_kernel(x_ref, halo_ref, g_ref, wg_ref, wu_ref, cw_ref, cb_ref, wd_ref, o_ref, conv_ref,
                       gs_ref, act_ref):
    t = pl.program_id(1)
    tm = x_ref.shape[1]
    x = x_ref[0]
    hb = _rmsnorm(x, g_ref[...]).astype(BF16)
    hh = jnp.where(t == 0, 0.0, _rmsnorm(halo_ref[0], g_ref[...])).astype(BF16)
    hcat = jnp.concatenate([hh, hb], axis=0)
    last = t == pl.num_programs(1) - 1
    for c in range(wg_ref.shape[1] // FF_CHUNK):
        cs = slice(c * FF_CHUNK, (c + 1) * FF_CHUNK)
        g0 = _ffn_chunk(cs, hcat, hb, lambda k, raw: raw, wg_ref, wu_ref, cw_ref, cb_ref, gs_ref, act_ref)

        @pl.when(last)
        def _():
            conv_ref[0, :, cs] = g0[tm - (CONV_W - 1):, :]
    o_ref[0] = x + _dot(act_ref[...], wd_ref[...])


def _ffn_sample_kernel(seq, x_ref, p1_ref, p2_ref, g_ref, wg_ref, wu_ref, cw_ref, cb_ref, wd_ref, o_ref, gate_ref,
                       gs_ref, act_ref):
    x = x_ref[...]
    tm = x.shape[0]
    hb = _rmsnorm(x, g_ref[...]).astype(BF16)
    hcat = jnp.concatenate([jnp.zeros((HALO, x.shape[1]), BF16), hb], axis=0)
    pos = lax.broadcasted_iota(jnp.int32, (tm, FF_CHUNK), 0) % seq
    for c in range(wg_ref.shape[1] // FF_CHUNK):
        cs = slice(c * FF_CHUNK, (c + 1) * FF_CHUNK)
        prev = lambda k, raw: jnp.where(pos < k, (p1_ref, p2_ref)[k - 1][:, cs], raw)
        gate_ref[:, cs] = _ffn_chunk(cs, hcat, hb, prev, wg_ref, wu_ref, cw_ref, cb_ref, gs_ref, act_ref)
    o_ref[...] = x + _dot(act_ref[...], wd_ref[...])


def _resident(a, n_grid):
    zeros = (0,) * a.ndim
    return pl.BlockSpec(a.shape, lambda *_: zeros, pipeline_mode=pl.Buffered(1))


def _ffn_weights(g, w_gate, w_up, conv_w, conv_b, w_down):
    return (g.reshape(1, -1), w_gate.astype(BF16), w_up.astype(BF16), conv_w, conv_b.reshape(1, -1),
            w_down.astype(BF16))


def _ffn_prompt(x, g, w_gate, w_up, conv_w, conv_b, w_down, *, tm):
    B, T, D = x.shape
    dff = w_gate.shape[1]
    ws = _ffn_weights(g, w_gate, w_up, conv_w, conv_b, w_down)
    hpt = tm // HALO
    return pl.pallas_call(
        _ffn_prompt_kernel, grid=(B, T // tm),
        in_specs=[pl.BlockSpec((1, tm, D), lambda b, t: (b, t, 0)),
                  pl.BlockSpec((1, HALO, D), lambda b, t: (b, jnp.maximum(t * hpt - 1, 0), 0))]
                 + [_resident(w, 2) for w in ws],
        out_specs=[pl.BlockSpec((1, tm, D), lambda b, t: (b, t, 0)),
                   pl.BlockSpec((1, CONV_W - 1, dff), lambda b, t: (b, 0, 0))],
        out_shape=[jax.ShapeDtypeStruct((B, T, D), F32), jax.ShapeDtypeStruct((B, CONV_W - 1, dff), F32)],
        scratch_shapes=[pltpu.VMEM((tm + HALO, FF_CHUNK), F32), pltpu.VMEM((tm, dff), BF16)],
        compiler_params=_cparams("parallel", "arbitrary"), name="ffn_prompt",
    )(x, x, *ws)


def _ffn_sample(x, state, g, w_gate, w_up, conv_w, conv_b, w_down):
    B, T, D = x.shape
    dff = w_gate.shape[1]
    ws = _ffn_weights(g, w_gate, w_up, conv_w, conv_b, w_down)
    zeros = jnp.zeros((B, T, dff), F32)
    p1 = zeros.at[:, 0].set(state[:, 1]).reshape(B * T, dff)
    p2 = zeros.at[:, 0].set(state[:, 0]).at[:, 1].set(state[:, 1]).reshape(B * T, dff)
    args = (x.reshape(B * T, D), p1, p2) + ws
    y, gate = pl.pallas_call(
        functools.partial(_ffn_sample_kernel, T), grid=(1,),
        in_specs=[_resident(a, 1) for a in args],
        out_specs=[pl.BlockSpec((B * T, D), lambda i: (0, 0)), pl.BlockSpec((B * T, dff), lambda i: (0, 0))],
        out_shape=[jax.ShapeDtypeStruct((B * T, D), F32), jax.ShapeDtypeStruct((B * T, dff), F32)],
        scratch_shapes=[pltpu.VMEM((B * T + HALO, FF_CHUNK), F32), pltpu.VMEM((B * T, dff), BF16)],
        compiler_params=_cparams("arbitrary"), name="ffn_sample",
    )(*args)
    return y.reshape(B, T, D), gate.reshape(B, T, dff)[:, T - (CONV_W - 1):]


def _proj_diff_kernel(x_ref, g_ref, w_ref, qn_ref, kn_ref, gmat_ref, q_ref, k_ref, v_ref, kb_ref, vb_ref):
    wq = q_ref.shape[2]
    hb = _rmsnorm(x_ref[0], g_ref[...]).astype(BF16)
    gmat = gmat_ref[...]
    q_ref[0] = (_head_rmsnorm(_dot(hb, w_ref[:, :wq]), gmat, qn_ref[...]) * QK_SCALE).astype(BF16)
    k = _head_rmsnorm(_dot(hb, w_ref[:, wq:2 * wq]), gmat, kn_ref[...])
    k_ref[0] = k
    kb_ref[0] = k.astype(BF16)
    v = _dot(hb, w_ref[:, 2 * wq:])
    v_ref[0] = v
    vb_ref[0] = v.astype(BF16)


def _proj_diff(x, g, w_in, qn, kn, *, tm):
    B, T, D = x.shape
    wv = w_in.shape[1] // 3
    wq = wv
    reps = wq // HEAD_DIM
    args = (x, g.reshape(1, D), w_in.astype(BF16), jnp.tile(qn, reps).reshape(1, wq),
            jnp.tile(kn, reps).reshape(1, wq), _head_mean_matrix())
    full = lambda a: pl.BlockSpec(a.shape, lambda b, t: (0,) * a.ndim)
    row = lambda w: pl.BlockSpec((1, tm, w), lambda b, t: (b, t, 0))
    sds = lambda w, dt: jax.ShapeDtypeStruct((B, T, w), dt)
    return pl.pallas_call(
        _proj_diff_kernel, grid=(B, T // tm),
        in_specs=[row(D)] + [full(a) for a in args[1:]],
        out_specs=[row(wq), row(wq), row(wv), row(wq), row(wv)],
        out_shape=[sds(wq, BF16), sds(wq, F32), sds(wv, F32), sds(wq, BF16), sds(wv, BF16)],
        compiler_params=_cparams("parallel", "parallel"), name="proj_diff",
    )(*args)


def _t5_bucket(rel):
    rel = np.asarray(rel)
    max_exact = N_BUCKETS // 2
    relf = np.maximum(rel, 1).astype(np.float32)
    large = max_exact + (np.log(relf / np.float32(max_exact)) / np.float32(math.log(MAX_DISTANCE / max_exact))
                         * np.float32(N_BUCKETS - max_exact)).astype(np.int32)
    large = np.minimum(large, N_BUCKETS - 1)
    return np.where(rel < max_exact, rel, large).astype(np.int32)


def _bias_from_buckets(bucket, rb_ref, h):
    out = jnp.zeros(bucket.shape, F32)
    for b in range(N_BUCKETS):
        out = jnp.where(bucket == b, rb_ref[b, h], out)
    return out


def _diff_lambda(lq1_ref, lk1_ref, lq2_ref, lk2_ref, lam_init):
    s1 = jnp.sum(lq1_ref[...] * lk1_ref[...], axis=1, keepdims=True)
    s2 = jnp.sum(lq2_ref[...] * lk2_ref[...], axis=1, keepdims=True)
    return jnp.exp(s1) - jnp.exp(s2) + lam_init


def _diff_prompt_kernel(lam_init, rb_ref, q_ref, k_ref, v_ref, bk_ref, lq1_ref, lk1_ref, lq2_ref, lk2_ref,
                        sub_ref, o_ref, m_ref, acc_ref, bias_ref):
    T = q_ref.shape[1]
    h = pl.program_id(1)
    lane = lax.broadcasted_iota(jnp.int32, (BLK, BLK), 1)
    row = lax.broadcasted_iota(jnp.int32, (BLK, BLK), 0)
    lo_half = _head0_mask(BF16)
    causal = lane <= row
    causal2 = jnp.concatenate([causal, causal], axis=1)
    lam =_diff_lambda(lq1_ref, lk1_ref, lq2_ref, lk2_ref, lam_init)
    for d in range(2):
        bias_ref[d] = _bias_from_buckets(bk_ref[d], rb_ref, h)
    far_bias = rb_ref[N_BUCKETS - 1, h]
    ones = jnp.ones((BLK, BLK), BF16)

    def tile(q2, j, bias, masked):
        ks = pl.ds(pl.multiple_of(j * BLK, BLK), BLK)
        s = _dot_nt(q2, _block_diag_pair(k_ref[0, ks, :], lo_half))
        s = s + bias
        if masked:
            s = jnp.where(causal2, s, NEG)
        mx = jnp.concatenate([jnp.max(s[:, :BLK], axis=1, keepdims=True),
                              jnp.max(s[:, BLK:], axis=1, keepdims=True)], axis=0)
        m_old = m_ref[...]
        m_new = jnp.maximum(m_old, mx)
        m_ref[...] = m_new
        p = jnp.exp(s - jnp.concatenate([m_new[:BLK], m_new[BLK:]], axis=1))
        alpha = jnp.exp(m_old - m_new)
        pst = jnp.concatenate([p[:, :BLK], p[:, BLK:]], axis=0).astype(BF16)
        vaug = jnp.concatenate([v_ref[0, ks, :], ones], axis=1)
        acc_ref[...] = jnp.concatenate([alpha, alpha], axis=1) * acc_ref[...] + _dot(pst, vaug)

    def qblock(i, carry):
        qs = pl.ds(pl.multiple_of(i * BLK, BLK), BLK)
        q2 = q_ref[0, qs, :]
        m_ref[...] = jnp.full_like(m_ref, NEG)
        acc_ref[...] = jnp.zeros_like(acc_ref)
        b0 = bias_ref[0]
        tile(q2, i, jnp.concatenate([b0, b0], axis=1), True)

        @pl.when(i >= 1)
        def _():
            b1 = bias_ref[1]
            tile(q2, i - 1, jnp.concatenate([b1, b1], axis=1), False)

        lax.fori_loop(0, jnp.maximum(i - 1, 0), lambda j, c: (tile(q2, j, far_bias, False), c)[1], 0)
        o_map = acc_ref[:, :BLK] / acc_ref[:, BLK:]
        o = o_map[:BLK] - lam * o_map[BLK:]
        o_ref[0, qs, :] = (_rmsnorm(o, sub_ref[...]) * (1.0 - lam_init)).astype(o_ref.dtype)
        return carry

    lax.fori_loop(0, T // BLK, qblock, 0)


def _diff_prompt(q, kb, vb, rel_bias, lq1, lk1, lq2, lk2, subln, lam_init):
    B, T, W = q.shape
    n_heads = W // LANES
    r = np.arange(BLK)[:, None]
    c = np.arange(BLK)[None, :]
    buckets = jnp.asarray(np.stack([_t5_bucket(np.maximum(r - c, 0)), _t5_bucket(BLK + r - c)]))
    spec = pl.BlockSpec((1, T, LANES), lambda b, h: (b, 0, h))
    small = lambda a: pl.BlockSpec(a.shape, lambda b, h: (0,) * a.ndim)
    vecs = [a.reshape(1, -1) for a in (lq1, lk1, lq2, lk2, subln)]
    return pl.pallas_call(
        functools.partial(_diff_prompt_kernel, lam_init), grid=(B, n_heads),
        in_specs=[pl.BlockSpec(memory_space=pltpu.SMEM)] + [spec] * 3 + [small(buckets)] + [small(a) for a in vecs],
        out_specs=spec,
        scratch_shapes=[pltpu.VMEM((2 * BLK, BLK), F32), pltpu.VMEM((2 * BLK, 2 * BLK), F32),
                        pltpu.VMEM((2, BLK, BLK), F32)],
        out_shape=jax.ShapeDtypeStruct((B, T, W), BF16),
        compiler_params=_cparams("parallel", "parallel"), name="diff_prompt",
    )(rel_bias, q, kb, vb, buckets, *vecs)


def _block_diag_queries(q, n_groups):
    B, T, W = q.shape
    qh = q.reshape(B, T, n_groups, HEAD_DIM).transpose(0, 2, 1, 3)
    eye = jnp.eye(n_groups, dtype=q.dtype)
    return (qh[:, :, :, None, :] * eye[None, :, None, :, None]).reshape(B, n_groups * T, W)


def _suffix_matrix():
    r = lax.broadcasted_iota(jnp.int32, (2 * BLK, 2 * BLK), 0) & (BLK - 1)
    c = lax.broadcasted_iota(jnp.int32, (2 * BLK, 2 * BLK), 1)
    return jnp.where((c >= BLK) | (r > c), 1.0, 0.0).astype(BF16)


def _pad_rows(x, n):
    return jnp.concatenate([x, jnp.zeros((n - x.shape[0],) + x.shape[1:], x.dtype)], axis=0)


def _own_head_columns(rows, seq, width):
    n_heads = rows.shape[1] // width
    col_head = lax.broadcasted_iota(jnp.int32, (seq, rows.shape[1]), 1) // width
    out = jnp.zeros((seq, rows.shape[1]), F32)
    for h in range(n_heads):
        out = jnp.where(col_head == h, rows[h * seq:(h + 1) * seq, :], out)
    return out


def _mix_sample_kernel(seq, n_sb, pt_ref, qbd_ref, kn_ref, vn_ref, lfn_ref, kp_ref, vp_ref, lfp_ref, o_ref,
                       run_ref, m_ref, l_ref, acc_ref):
    j = pl.program_id(1)
    sb = n_sb * seq
    n_fox = (BLK - sb) // seq
    lane = lax.broadcasted_iota(jnp.int32, (BLK, BLK), 1)
    row = lax.broadcasted_iota(jnp.int32, (BLK, BLK), 0)
    suffix = _suffix_matrix()

    def block(kb, vb, lft, mask):
        z = _dot_nt(qbd_ref[0], kb)
        zs, zf = z[:sb], z[sb:]
        ls = _log_sigmoid(zs)
        elf = jnp.concatenate([jnp.broadcast_to(lft[h:h + 1, :], (seq, BLK)) for h in range(n_fox)], axis=0)
        x = jnp.concatenate([ls - zs, elf], axis=0)
        if mask is not None:
            x = jnp.where(mask, x, 0.0)
        hi = x.astype(BF16)
        lo = (x - hi.astype(F32)).astype(BF16)
        cs = _dot(jnp.concatenate([hi, lo], axis=1), suffix)
        run = run_ref[...]
        run_ref[...] = run + cs[:, BLK:]
        later = cs[:, :BLK] + run
        a = jnp.exp(ls + later[:sb])
        s = zf + later[sb:]
        if mask is not None:
            a = jnp.where(mask[:sb], a, 0.0)
            s = jnp.where(mask[sb:], s, NEG)
        m_old = m_ref[...]
        m_new = jnp.maximum(m_old, jnp.max(s, axis=1, keepdims=True))
        m_ref[...] = m_new
        p = jnp.exp(s - m_new)
        alpha = jnp.exp(m_old - m_new)
        l_ref[...] = alpha * l_ref[...] + jnp.sum(p, axis=1, keepdims=True)
        pv = _dot(jnp.concatenate([a, p], axis=0).astype(BF16), vb)
        acc_ref[:sb, :] += pv[:sb]
        acc_ref[sb:, :] = jnp.concatenate([alpha] * (pv.shape[1] // BLK), axis=1) * acc_ref[sb:, :] + pv[sb:]

    @pl.when(j == 0)
    def _():
        run_ref[...] = jnp.zeros_like(run_ref)
        m_ref[...] = jnp.full_like(m_ref, NEG)
        l_ref[...] = jnp.zeros_like(l_ref)
        acc_ref[...] = jnp.zeros_like(acc_ref)
        t = row % seq
        own = lane < t + jnp.where(row < sb, 0, 1)
        block(_pad_rows(kn_ref[0], BLK).astype(BF16), _pad_rows(vn_ref[0], BLK).astype(BF16), lfn_ref[0], own)

    block(kp_ref[...].astype(BF16), vp_ref[...].astype(BF16), lfp_ref[...], None)

    @pl.when(j == pl.num_programs(1) - 1)
    def _():
        rows = jnp.concatenate([acc_ref[:sb, :], acc_ref[sb:, :] / l_ref[:, 0:1]], axis=0)
        o_ref[0] = _own_head_columns(rows, seq, HEAD_DIM)


def _mix_sample(q, k_new, v_new, logf_new, cache_k, cache_v, cache_logf, layer, page_table, n_sb):
    B, T, W = q.shape
    n_heads = W // HEAD_DIM
    n_fox = n_heads - n_sb
    assert n_heads * T == BLK and cache_k.shape[2] == BLK
    n_pages = page_table.shape[1]
    L, P = cache_k.shape[:2]
    lfn = jnp.zeros((B, n_fox, BLK), F32).at[:, :, :T].set(logf_new.transpose(0, 2, 1))
    lfp = cache_logf.transpose(0, 1, 3, 2)
    per_seq = lambda a: pl.BlockSpec((1,) + a.shape[1:], lambda b, j, pt: (b, 0, 0))
    paged = lambda r, c: pl.BlockSpec((None, None, r, c), lambda b, j, pt: (layer, pt[b, n_pages - 1 - j], 0, 0))
    args = (_block_diag_queries(q, n_heads), k_new, v_new, lfn)
    return pl.pallas_call(
        functools.partial(_mix_sample_kernel, T, n_sb),
        grid_spec=pltpu.PrefetchScalarGridSpec(
            num_scalar_prefetch=1, grid=(B, n_pages),
            in_specs=[per_seq(a) for a in args] + [paged(BLK, W), paged(BLK, W), paged(n_fox, BLK)],
            out_specs=pl.BlockSpec((1, T, W), lambda b, j, pt: (b, 0, 0)),
            scratch_shapes=[pltpu.VMEM((BLK, BLK), F32), pltpu.VMEM((n_fox * T, BLK), F32),
                            pltpu.VMEM((n_fox * T, BLK), F32), pltpu.VMEM((BLK, W), F32)]),
        out_shape=jax.ShapeDtypeStruct((B, T, W), F32),
        compiler_params=_cparams("parallel", "arbitrary"), name="mix_sample",
    )(page_table, *args, cache_k.reshape(L, P, BLK, W), cache_v.reshape(L, P, BLK, W), lfp)


def _diff_sample_kernel(seq, lam_init, pt_ref, qbd_ref, kn_ref, vn_ref, bk_ref, rbr_ref, kp_ref, vp_ref,
                        lq1_ref, lk1_ref, lq2_ref, lk2_ref, sub_ref, o_ref, m_ref, l_ref, acc_ref, far_ref):
    j = pl.program_id(1)
    lane = lax.broadcasted_iota(jnp.int32, (BLK, BLK), 1)
    row = lax.broadcasted_iota(jnp.int32, (BLK, BLK), 0)

    def bias_tile(bucket):
        out = jnp.zeros((BLK, BLK), F32)
        for b in range(N_BUCKETS):
            out = jnp.where(bucket == b, rbr_ref[:, b:b + 1], out)
        return out

    def block(kb, vb, bias, mask):
        s = _dot_nt(qbd_ref[0], kb) + bias
        if mask is not None:
            s = jnp.where(mask, s, NEG)
        m_old = m_ref[...]
        m_new = jnp.maximum(m_old, jnp.max(s, axis=1, keepdims=True))
        m_ref[...] = m_new
        p = jnp.exp(s - m_new)
        alpha = jnp.exp(m_old - m_new)
        l_ref[...] = alpha * l_ref[...] + jnp.sum(p, axis=1, keepdims=True)
        pv = _dot(p.astype(BF16), vb)
        acc_ref[...] = jnp.concatenate([alpha] * (pv.shape[1] // BLK), axis=1) * acc_ref[...] + pv

    @pl.when(j == 0)
    def _():
        m_ref[...] = jnp.full_like(m_ref, NEG)
        l_ref[...] = jnp.zeros_like(l_ref)
        acc_ref[...] = jnp.zeros_like(acc_ref)
        far_ref[...] = jnp.broadcast_to(rbr_ref[:, N_BUCKETS - 1:N_BUCKETS], (BLK, BLK))
        block(_pad_rows(kn_ref[0], BLK).astype(BF16), _pad_rows(vn_ref[0], BLK).astype(BF16), bias_tile(bk_ref[0]),
              lane <= row % seq)
        block(kp_ref[...].astype(BF16), vp_ref[...].astype(BF16), bias_tile(bk_ref[1]), None)

    @pl.when(j > 0)
    def _():
        block(kp_ref[...].astype(BF16), vp_ref[...].astype(BF16), far_ref[...], None)

    @pl.when(j == pl.num_programs(1) - 1)
    def _():
        rows = acc_ref[...] / l_ref[:, 0:1]
        col_head = lax.broadcasted_iota(jnp.int32, (seq, rows.shape[1]), 1) // BLK
        o_map = []
        for mp in range(2):
            om = jnp.zeros((seq, rows.shape[1]), F32)
            for h in range(rows.shape[1] // BLK):
                r0 = (2 * h + mp) * seq
                om = jnp.where(col_head == h, rows[r0:r0 + seq, :], om)
            o_map.append(om)
        lam = _diff_lambda(lq1_ref, lk1_ref, lq2_ref, lk2_ref, lam_init)
        o = o_map[0] - lam * o_map[1]
        sub = sub_ref[...]
        o_ref[0] = jnp.concatenate(
            [_rmsnorm(o[:, h * BLK:(h + 1) * BLK], sub) for h in range(rows.shape[1] // BLK)], axis=1) * (1.0 - lam_init)


def _diff_sample(q, k_new, v_new, cache_k, cache_v, layer, page_table, rel_bias, lq1, lk1, lq2, lk2, subln, lam_init):
    B, T, W = q.shape
    n_maps = W // HEAD_DIM
    assert n_maps * T == BLK and cache_k.shape[2] == BLK
    n_pages = page_table.shape[1]
    L, P = cache_k.shape[:2]
    t = (np.arange(BLK) % T)[:, None]
    c = np.arange(BLK)[None, :]
    buckets = jnp.asarray(np.stack([_t5_bucket(np.maximum(t - c, 0)), _t5_bucket(BLK + t - c)]))
    rb_rows = jnp.repeat(rel_bias.T, 2 * T, axis=0)
    per_seq = lambda a: pl.BlockSpec((1,) + a.shape[1:], lambda b, j, pt: (b, 0, 0))
    small = lambda a: pl.BlockSpec(a.shape, lambda b, j, pt: (0,) * a.ndim)
    paged = pl.BlockSpec((None, None, BLK, W), lambda b, j, pt: (layer, pt[b, n_pages - 1 - j], 0, 0))
    vecs = [a.reshape(1, -1) for a in (lq1, lk1, lq2, lk2, subln)]
    args = (_block_diag_queries(q, n_maps), k_new, v_new)
    return pl.pallas_call(
        functools.partial(_diff_sample_kernel, T, lam_init),
        grid_spec=pltpu.PrefetchScalarGridSpec(
            num_scalar_prefetch=1, grid=(B, n_pages),
            in_specs=[per_seq(a) for a in args] + [small(buckets), small(rb_rows), paged, paged]
                     + [small(a) for a in vecs],
            out_specs=pl.BlockSpec((1, T, W), lambda b, j, pt: (b, 0, 0)),
            scratch_shapes=[pltpu.VMEM((BLK, BLK), F32), pltpu.VMEM((BLK, BLK), F32), pltpu.VMEM((BLK, W), F32),
                            pltpu.VMEM((BLK, BLK), F32)]),
        out_shape=jax.ShapeDtypeStruct((B, T, W), F32),
        compiler_params=_cparams("parallel", "arbitrary"), name="diff_sample",
    )(page_table, *args, buckets, rb_rows, cache_k.reshape(L, P, BLK, W), cache_v.reshape(L, P, BLK, W), *vecs)


PROMPT_ROW_TILE = 512


def kernel(x_prompt, x_sample, cache_k_mix, cache_v_mix, cache_logf_mix, cache_k_diff, cache_v_diff, state_ffn_conv, page_table, rel_bias, norm_mix, w_in_mix, b_forget, qnorm_fox, knorm_fox, w_out_mix, norm_diff, w_in_diff, qnorm_diff, knorm_diff, lambda_q1, lambda_k1, lambda_q2, lambda_k2, subln_diff, w_out_diff, norm_ffn, w_gate, w_up, conv_w, conv_b, w_down):
    B, T, D = x_prompt.shape
    DB, DT, _ = x_sample.shape
    depth = norm_ffn.shape[0]
    n_fox = b_forget.shape[1]
    n_mix = cache_k_mix.shape[3]
    n_sb = n_mix - n_fox
    n_diff = cache_k_diff.shape[3]
    sb_pairs = n_sb * HEAD_DIM // LANES
    fox_pairs = n_fox * HEAD_DIM // LANES
    tm = min(T, PROMPT_ROW_TILE)
    xp, xs = x_prompt, x_sample
    outs = {n: [] for n in ("kmp", "vmp", "lfp", "kms", "vms", "lfs", "kdp", "vdp", "kds", "vds", "cp", "cs")}
    for l in range(depth):
        j = l // 2
        if l % 2 == 0:
            pw = (norm_mix[j], w_in_mix[j], b_forget[j], qnorm_fox[j], knorm_fox[j])
            q, k, v, kb, vb, logf, c, ct = _proj_mix(xp, *pw, tm=tm, with_cumsum=True)
            o_sb = _sb_prompt(q, kb, vb, sb_pairs)
            o_fx = _fox_prompt(q, kb, vb, c, ct, sb_pairs, fox_pairs)
            xp = _out_proj(xp.reshape(B * T, D), [o_sb.reshape(B * T, -1), o_fx.reshape(B * T, -1)], w_out_mix[j],
                           tm=tm).reshape(B, T, D)
            outs["kmp"].append(k.reshape(B, T, n_mix, HEAD_DIM))
            outs["vmp"].append(v.reshape(B, T, n_mix, HEAD_DIM))
            outs["lfp"].append(logf)
            qs, ks, vs, _, _, lfs = _proj_mix(xs.reshape(1, DB * DT, D), *pw, tm=DB * DT, with_cumsum=False)
            ks, vs, lfs = ks.reshape(DB, DT, -1), vs.reshape(DB, DT, -1), lfs.reshape(DB, DT, n_fox)
            o = _mix_sample(qs.reshape(DB, DT, -1), ks, vs, lfs, cache_k_mix, cache_v_mix, cache_logf_mix, j,
                            page_table, n_sb)
            xs = _out_proj(xs.reshape(DB * DT, D), [o.reshape(DB * DT, -1).astype(BF16)], w_out_mix[j],
                           tm=DB * DT).reshape(DB, DT, D)
            outs["kms"].append(ks.reshape(DB, DT, n_mix, HEAD_DIM))
            outs["vms"].append(vs.reshape(DB, DT, n_mix, HEAD_DIM))
            outs["lfs"].append(lfs)
        else:
            lam_init = 0.8 - 0.6 * math.exp(-0.3 * l)
            pw = (norm_diff[j], w_in_diff[j], qnorm_diff[j], knorm_diff[j])
            lam_w = (lambda_q1[j], lambda_k1[j], lambda_q2[j], lambda_k2[j], subln_diff[j], lam_init)
            q, k, v, kb, vb = _proj_diff(xp, *pw, tm=tm)
            o = _diff_prompt(q, kb, vb, rel_bias, *lam_w)
            xp = _out_proj(xp.reshape(B * T, D), [o.reshape(B * T, -1)], w_out_diff[j], tm=tm).reshape(B, T, D)
            outs["kdp"].append(k.reshape(B, T, n_diff, 2, HEAD_DIM))
            outs["vdp"].append(v.reshape(B, T, n_diff, 2 * HEAD_DIM))
            qs, ks, vs, _, _ = _proj_diff(xs.reshape(1, DB * DT, D), *pw, tm=DB * DT)
            ks, vs = ks.reshape(DB, DT, -1), vs.reshape(DB, DT, -1)
            o = _diff_sample(qs.reshape(DB, DT, -1), ks, vs, cache_k_diff, cache_v_diff, j, page_table, rel_bias,
                             *lam_w)
            xs = _out_proj(xs.reshape(DB * DT, D), [o.reshape(DB * DT, -1).astype(BF16)], w_out_diff[j],
                           tm=DB * DT).reshape(DB, DT, D)
            outs["kds"].append(ks.reshape(DB, DT, n_diff, 2, HEAD_DIM))
            outs["vds"].append(vs.reshape(DB, DT, n_diff, 2 * HEAD_DIM))
        fw = (norm_ffn[l], w_gate[l], w_up[l], conv_w[l], conv_b[l], w_down[l])
        xp, cp = _ffn_prompt(xp, *fw, tm=tm)
        xs, cs = _ffn_sample(xs, state_ffn_conv[l], *fw)
        outs["cp"].append(cp)
        outs["cs"].append(cs)
    st = {n: jnp.stack(a) for n, a in outs.items()}
    return (xp, xs, st["kmp"], st["vmp"], st["lfp"], st["kms"], st["vms"], st["lfs"], st["kdp"], st["vdp"],
            st["kds"], st["vds"], st["cp"], st["cs"])
```

```python
import functools
import math

import numpy as np
import jax
import jax.numpy as jnp
from jax import lax
from jax.experimental import pallas as pl
from jax.experimental.pallas import tpu as pltpu

F32 = jnp.float32
BF16 = jnp.bfloat16

HEAD_DIM = 64
N_BUCKETS = 32
MAX_DISTANCE = 128
CONV_W = 3
EPS = 1e-6
QK_SCALE = HEAD_DIM ** -0.5

LANES = 128
SUBLANES = 8
MXU_DIM = 256
VMEM_LIMIT = 56 * 1024 * 1024

BLK = LANES
NEG = -1e30
EXP_ZERO_BELOW = -104.0

PROMPT_ROW_TILE = 512
ATT_BLOCK = 512
PAGES_PER_STEP = 4


def _cparams(*sem):
    return pltpu.CompilerParams(dimension_semantics=sem, vmem_limit_bytes=VMEM_LIMIT)


def _dot(a, b):
    return jnp.dot(a, b, preferred_element_type=F32)


def _dot_nt(a, b):
    return lax.dot_general(a, b, (((1,), (1,)), ((), ())), preferred_element_type=F32)


def _dot_f32(a, b):
    return jnp.dot(a, b, preferred_element_type=F32, precision=lax.Precision.HIGHEST)


def _split(a):
    hi = a.astype(BF16)
    lo = (a - hi.astype(F32)).astype(BF16)
    return jnp.concatenate([hi, lo], axis=1)


def _log_sigmoid(x):
    return jnp.minimum(x, 0.0) - jnp.log1p(jnp.exp(-jnp.abs(x)))


def _rmsnorm(x, g):
    return x * lax.rsqrt(jnp.mean(x * x, axis=-1, keepdims=True) + EPS) * g


def _head_rmsnorm(y, gmat2, gain):
    outs = []
    for c in range(y.shape[1] // LANES):
        yc = y[:, c * LANES:(c + 1) * LANES]
        outs.append(yc * lax.rsqrt(_dot(_split(yc * yc), gmat2) + EPS))
    return jnp.concatenate(outs, axis=1) * gain


def _head_mean_matrix():
    idx = np.arange(LANES) // HEAD_DIM
    g = (idx[:, None] == idx[None, :]).astype(np.float32) / HEAD_DIM
    return jnp.asarray(np.concatenate([g, g], axis=0), BF16)


def _suffix_matrix():
    r = lax.broadcasted_iota(jnp.int32, (2 * BLK, 2 * BLK), 0) & (BLK - 1)
    c = lax.broadcasted_iota(jnp.int32, (2 * BLK, 2 * BLK), 1)
    return jnp.where((c >= BLK) | (r > c), 1.0, 0.0).astype(BF16)


def _head0_mask(dtype, rows=BLK):
    lane = lax.broadcasted_iota(jnp.int32, (rows, LANES), 1)
    if dtype == F32:
        return lane < HEAD_DIM
    return jnp.where(lane < HEAD_DIM, 1.0, 0.0).astype(dtype) > 0


def _block_diag_pair(x, lo_half):
    zero = jnp.zeros_like(x)
    return jnp.concatenate([jnp.where(lo_half, x, zero), jnp.where(lo_half, zero, x)], axis=0)


def _pad_rows(x, n):
    return jnp.concatenate([x, jnp.zeros((n - x.shape[0],) + x.shape[1:], x.dtype)], axis=0)


def _proj_mix_kernel(with_cumsum, x_ref, g_ref, w_ref, wf_ref, wft_ref, bfr_ref, bfc_ref, qn_ref, kn_ref,
                     gmat_ref, q_ref, k_ref, v_ref, kb_ref, vb_ref, logf_ref, *rest):
    ws = q_ref.shape[2] // 2
    n_fox = logf_ref.shape[2]
    tm = x_ref.shape[1]
    hb = _rmsnorm(x_ref[0], g_ref[...]).astype(BF16)

    def sec(i):
        return _dot(hb, w_ref[:, i * ws:(i + 1) * ws])

    q_ref[0, :, :ws] = (sec(0) * QK_SCALE).astype(BF16)
    k_sb = sec(1)
    k_ref[0, :, :ws] = k_sb
    kb_ref[0, :, :ws] = k_sb.astype(BF16)
    v_sb = sec(2)
    v_ref[0, :, :ws] = v_sb
    vb_ref[0, :, :ws] = v_sb.astype(BF16)
    gmat = gmat_ref[...]
    q_ref[0, :, ws:] = (_head_rmsnorm(sec(3), gmat, qn_ref[...]) * QK_SCALE).astype(BF16)
    k_fx = _head_rmsnorm(sec(4), gmat, kn_ref[...])
    k_ref[0, :, ws:] = k_fx
    kb_ref[0, :, ws:] = k_fx.astype(BF16)
    v_fx = sec(5)
    v_ref[0, :, ws:] = v_fx
    vb_ref[0, :, ws:] = v_fx.astype(BF16)

    logf = _log_sigmoid(_dot(hb, wf_ref[...]) + bfr_ref[...])
    logf_ref[0] = logf[:, :n_fox]
    if not with_cumsum:
        return
    c_ref, ct_ref, carry_ref, carryt_ref = rest

    @pl.when(pl.program_id(1) == 0)
    def _():
        carry_ref[...] = jnp.zeros_like(carry_ref)
        carryt_ref[...] = jnp.zeros_like(carryt_ref)

    r = lax.broadcasted_iota(jnp.int32, (tm, tm), 0)
    c = lax.broadcasted_iota(jnp.int32, (tm, tm), 1)
    csum = _dot_f32(jnp.where(c <= r, 1.0, 0.0).astype(F32), logf) + carry_ref[...]
    c_ref[0] = csum[:, :n_fox]
    carry_ref[...] = csum[tm - 1:tm, :]
    logft = _log_sigmoid(_dot_nt(wft_ref[...], hb) + bfc_ref[:, 0:1])
    csumt = _dot_f32(logft, jnp.where(r <= c, 1.0, 0.0).astype(F32)) + carryt_ref[:, 0:1]
    ct_ref[0] = csumt[:n_fox, :]
    carryt_ref[...] = jnp.broadcast_to(csumt[:, tm - 1:tm], carryt_ref.shape)


def _proj_mix(x, g, w_in, b_f, qn, kn, *, tm, with_cumsum):
    B, T, D = x.shape
    n_fox = b_f.shape[0]
    ws = (w_in.shape[1] - n_fox) // 6
    wmain = w_in[:, :6 * ws].astype(BF16)
    wf = jnp.zeros((D, LANES), BF16).at[:, :n_fox].set(w_in[:, 6 * ws:].astype(BF16))
    wft = jnp.zeros((2 * SUBLANES, D), BF16).at[:n_fox, :].set(w_in[:, 6 * ws:].T.astype(BF16))
    bfr = jnp.zeros((1, LANES), F32).at[0, :n_fox].set(b_f)
    bfc = jnp.zeros((2 * SUBLANES, LANES), F32).at[:n_fox, :].set(jnp.broadcast_to(b_f[:, None], (n_fox, LANES)))
    reps = ws // HEAD_DIM
    full = lambda a: pl.BlockSpec(a.shape, lambda b, t: (0,) * a.ndim)
    row = lambda w: pl.BlockSpec((1, tm, w), lambda b, t: (b, t, 0))
    args = (x, g.reshape(1, D), wmain, wf, wft, bfr, bfc, jnp.tile(qn, reps).reshape(1, ws),
            jnp.tile(kn, reps).reshape(1, ws), _head_mean_matrix())
    out_shape = [jax.ShapeDtypeStruct((B, T, 2 * ws), BF16), jax.ShapeDtypeStruct((B, T, 2 * ws), F32),
                 jax.ShapeDtypeStruct((B, T, 2 * ws), F32), jax.ShapeDtypeStruct((B, T, 2 * ws), BF16),
                 jax.ShapeDtypeStruct((B, T, 2 * ws), BF16), jax.ShapeDtypeStruct((B, T, n_fox), F32)]
    out_specs = [row(2 * ws)] * 5 + [row(n_fox)]
    scratch = []
    if with_cumsum:
        out_shape += [jax.ShapeDtypeStruct((B, T, n_fox), F32), jax.ShapeDtypeStruct((B, n_fox, T), F32)]
        out_specs += [row(n_fox), pl.BlockSpec((1, n_fox, tm), lambda b, t: (b, 0, t))]
        scratch = [pltpu.VMEM((1, LANES), F32), pltpu.VMEM((2 * SUBLANES, LANES), F32)]
    return pl.pallas_call(
        functools.partial(_proj_mix_kernel, with_cumsum),
        grid=(B, T // tm),
        in_specs=[row(D)] + [full(a) for a in args[1:]],
        out_specs=out_specs, out_shape=out_shape, scratch_shapes=scratch,
        compiler_params=_cparams("parallel", "arbitrary"), name="proj_mix",
    )(*args)


def _sb_prompt_kernel(q_ref, k_ref, v_ref, o_ref, r_ref, acc_ref):
    T = q_ref.shape[1]
    lane = lax.broadcasted_iota(jnp.int32, (BLK, BLK), 1)
    row = lax.broadcasted_iota(jnp.int32, (BLK, BLK), 0)
    lo_half = _head0_mask(BF16)
    strict = lane < row
    strict2 = jnp.concatenate([strict, strict], axis=1)
    suffix = _suffix_matrix()

    def tile(q2, j, masked):
        ks = pl.ds(pl.multiple_of(j * BLK, BLK), BLK)
        z = _dot_nt(q2, _block_diag_pair(k_ref[0, ks, :], lo_half))
        ls = _log_sigmoid(z)
        l1m = ls - z
        if masked:
            l1m = jnp.where(strict2, l1m, 0.0)
        a = []
        for h in range(2):
            hs = slice(h * BLK, (h + 1) * BLK)
            cs = _dot(_split(l1m[:, hs]), suffix)
            run = r_ref[:, hs]
            ah = jnp.exp(ls[:, hs] + cs[:, :BLK] + run)
            if masked:
                ah = jnp.where(strict, ah, 0.0)
            a.append(ah.astype(BF16))
            r_ref[:, hs] = run + cs[:, BLK:]
        acc_ref[...] += _dot(jnp.concatenate(a, axis=1), _block_diag_pair(v_ref[0, ks, :], lo_half))

    def qblock(i, carry):
        qs = pl.ds(pl.multiple_of(i * BLK, BLK), BLK)
        q2 = q_ref[0, qs, :]
        r_ref[...] = jnp.zeros_like(r_ref)
        acc_ref[...] = jnp.zeros_like(acc_ref)
        tile(q2, i, True)

        def alive():
            return (jnp.max(r_ref[...]) > EXP_ZERO_BELOW).astype(jnp.int32)

        def cond(s):
            return (s[0] >= 0) & (s[1] > 0)

        def body(s):
            tile(q2, s[0], False)
            return s[0] - 1, alive()

        lax.while_loop(cond, body, (i - 1, alive()))
        o_ref[0, qs, :] = acc_ref[...].astype(o_ref.dtype)
        return carry

    lax.fori_loop(0, T // BLK, qblock, 0)


def _sb_prompt(q, kb, vb, n_pairs):
    B, T, _ = q.shape
    spec = pl.BlockSpec((1, T, LANES), lambda b, p: (b, 0, p))
    return pl.pallas_call(
        _sb_prompt_kernel, grid=(B, n_pairs), in_specs=[spec] * 3, out_specs=spec,
        out_shape=jax.ShapeDtypeStruct((B, T, n_pairs * LANES), BF16),
        scratch_shapes=[pltpu.VMEM((BLK, 2 * BLK), F32), pltpu.VMEM((BLK, BLK), F32)],
        compiler_params=_cparams("parallel", "parallel"), name="sb_prompt",
    )(q, kb, vb)


def _softmax_step(s, m_ref, l_ref, idx):
    n_chunks = s.shape[1] // LANES
    chunk_max = s[:, :LANES]
    for c in range(1, n_chunks):
        chunk_max = jnp.maximum(chunk_max, s[:, c * LANES:(c + 1) * LANES])
    m_old = m_ref[idx]
    m_new = jnp.maximum(m_old, jnp.max(chunk_max, axis=1, keepdims=True))
    m_ref[idx] = m_new
    p = jnp.exp(s - jnp.concatenate([m_new] * n_chunks, axis=1))
    alpha = jnp.exp(m_old - m_new)
    lane_sum = p[:, :LANES]
    for c in range(1, n_chunks):
        lane_sum = lane_sum + p[:, c * LANES:(c + 1) * LANES]
    l_ref[idx] = alpha * l_ref[idx] + lane_sum
    return p, alpha


def _causal_mask(tb):
    return (lax.broadcasted_iota(jnp.int32, (tb, tb), 1) <= lax.broadcasted_iota(jnp.int32, (tb, tb), 0))


def _fox_prompt_kernel(tb, q_ref, k_ref, v_ref, c_ref, ct_ref, o_ref, kbd_ref, vbd_ref, m_ref, l_ref, acc_ref):
    T = q_ref.shape[1]
    lo16 = _head0_mask(BF16, tb)
    lo32 = _head0_mask(F32, tb)

    def prepare(j, carry):
        ks = pl.ds(pl.multiple_of(j * tb, tb), tb)
        kbd_ref[j] = _block_diag_pair(k_ref[0, ks, :], lo16)
        vbd_ref[j] = _block_diag_pair(v_ref[0, ks, :], lo16)
        return carry

    lax.fori_loop(0, T // tb, prepare, 0)

    def tile(q2, cq, j, masked):
        ks = pl.ds(pl.multiple_of(j * tb, tb), tb)
        z = _dot_nt(q2, kbd_ref[j])
        ps, alphas = [], []
        for h in range(2):
            s = z[:, h * tb:(h + 1) * tb] + cq[h] - ct_ref[0, 0, h:h + 1, ks]
            if masked:
                s = jnp.where(_causal_mask(tb), s, NEG)
            p, alpha = _softmax_step(s, m_ref, l_ref, h)
            ps.append(p.astype(BF16))
            alphas.append(alpha)
        pv = _dot(jnp.concatenate(ps, axis=1), vbd_ref[j])
        acc_ref[...] = jnp.where(lo32, alphas[0], alphas[1]) * acc_ref[...] + pv

    def qblock(i, carry):
        qs = pl.ds(pl.multiple_of(i * tb, tb), tb)
        q2 = q_ref[0, qs, :]
        cq = [c_ref[0, 0, qs, h:h + 1] for h in range(2)]
        m_ref[...] = jnp.full_like(m_ref, NEG)
        l_ref[...] = jnp.zeros_like(l_ref)
        acc_ref[...] = jnp.zeros_like(acc_ref)
        tile(q2, cq, i, True)
        lax.fori_loop(0, i, lambda j, c: (tile(q2, cq, j, False), c)[1], 0)
        denom = jnp.where(lo32, jnp.sum(l_ref[0], axis=1, keepdims=True), jnp.sum(l_ref[1], axis=1, keepdims=True))
        o_ref[0, qs, :] = (acc_ref[...] / denom).astype(o_ref.dtype)
        return carry

    lax.fori_loop(0, T // tb, qblock, 0)


def _fox_prompt(q, kb, vb, c, ct, first_pair, n_pairs, *, tb):
    B, T, _ = q.shape
    spec = pl.BlockSpec((1, T, LANES), lambda b, p: (b, 0, first_pair + p))
    c4 = c.reshape(B, T, n_pairs, 2).transpose(0, 2, 1, 3)
    ct4 = ct.reshape(B, n_pairs, 2, T)
    return pl.pallas_call(
        functools.partial(_fox_prompt_kernel, tb), grid=(B, n_pairs),
        in_specs=[spec] * 3 + [pl.BlockSpec((1, 1, T, 2), lambda b, p: (b, p, 0, 0)),
                               pl.BlockSpec((1, 1, 2, T), lambda b, p: (b, p, 0, 0))],
        out_specs=pl.BlockSpec((1, T, LANES), lambda b, p: (b, 0, p)),
        out_shape=jax.ShapeDtypeStruct((B, T, n_pairs * LANES), BF16),
        scratch_shapes=[pltpu.VMEM((T // tb, 2 * tb, LANES), BF16), pltpu.VMEM((T // tb, 2 * tb, LANES), BF16),
                        pltpu.VMEM((2, tb, LANES), F32), pltpu.VMEM((2, tb, LANES), F32),
                        pltpu.VMEM((tb, LANES), F32)],
        compiler_params=_cparams("parallel", "parallel"), name="fox_prompt",
    )(q, kb, vb, c4, ct4)


def _out_proj_kernel(n_parts, x_ref, *refs):
    o_ref = refs[-1]
    y = x_ref[...]
    for i in range(n_parts):
        y = y + _dot(refs[i][...], refs[n_parts + i][...])
    o_ref[...] = y


def _out_proj(x2, parts, w_out, *, tm):
    M, D = x2.shape
    ws, off = [], 0
    for p in parts:
        ws.append(w_out[off:off + p.shape[1]].astype(BF16))
        off += p.shape[1]
    n = len(parts)
    return pl.pallas_call(
        functools.partial(_out_proj_kernel, n), grid=(M // tm,),
        in_specs=[pl.BlockSpec((tm, D), lambda i: (i, 0))]
                 + [pl.BlockSpec((tm, p.shape[1]), lambda i: (i, 0)) for p in parts]
                 + [pl.BlockSpec(w.shape, lambda i: (0, 0)) for w in ws],
        out_specs=pl.BlockSpec((tm, D), lambda i: (i, 0)),
        out_shape=jax.ShapeDtypeStruct((M, D), F32),
        compiler_params=_cparams("parallel"), name="out_proj",
    )(x2, *parts, *ws)


FF_CHUNK = MXU_DIM
HALO = 2 * SUBLANES


def _silu(x):
    return x * (1.0 / (1.0 + jnp.exp(-x)))


def _ffn_chunk(cs, hcat, hb, g_prev, wg_ref, wu_ref, cw_ref, cb_ref, gs_ref, act_ref):
    tm = hb.shape[0]
    gx = _dot(hcat, wg_ref[:, cs])
    gs_ref[...] = gx
    g0 = gx[gx.shape[0] - tm:]
    g1 = g_prev(1, gs_ref[HALO - 1:HALO - 1 + tm, :])
    g2 = g_prev(2, gs_ref[HALO - 2:HALO - 2 + tm, :])
    gc = cb_ref[:, cs] + cw_ref[0:1, cs] * g2
    gc = gc + cw_ref[1:2, cs] * g1
    gc = gc + cw_ref[2:3, cs] * g0
    act_ref[:, cs] = (_silu(gc) * _dot(hb, wu_ref[:, cs])).astype(BF16)
    return g0


def _ffn_prompt_kernel(x_ref, halo_ref, g_ref, wg_ref, wu_ref, cw_ref, cb_ref, wd_ref, o_ref, conv_ref,
                       gs_ref, act_ref):
    t = pl.program_id(1)
    tm = x_ref.shape[1]
    x = x_ref[0]
    hb = _rmsnorm(x, g_ref[...]).astype(BF16)
    hh = jnp.where(t == 0, 0.0, _rmsnorm(halo_ref[0], g_ref[...])).astype(BF16)
    hcat = jnp.concatenate([hh, hb], axis=0)
    last = t == pl.num_programs(1) - 1
    for c in range(wg_ref.shape[1] // FF_CHUNK):
        cs = slice(c * FF_CHUNK, (c + 1) * FF_CHUNK)
        g0 = _ffn_chunk(cs, hcat, hb, lambda k, raw: raw, wg_ref, wu_ref, cw_ref, cb_ref, gs_ref, act_ref)

        @pl.when(last)
        def _():
            conv_ref[0, :, cs] = g0[tm - (CONV_W - 1):, :]
    o_ref[0] = x + _dot(act_ref[...], wd_ref[...])


def _ffn_sample_kernel(seq, x_ref, p1_ref, p2_ref, g_ref, wg_ref, wu_ref, cw_ref, cb_ref, wd_ref, o_ref, gate_ref,
                       gs_ref, act_ref):
    x = x_ref[...]
    tm = x.shape[0]
    hb = _rmsnorm(x, g_ref[...]).astype(BF16)
    hcat = jnp.concatenate([jnp.zeros((HALO, x.shape[1]), BF16), hb], axis=0)
    pos = lax.broadcasted_iota(jnp.int32, (tm, FF_CHUNK), 0) % seq
    for c in range(wg_ref.shape[1] // FF_CHUNK):
        cs = slice(c * FF_CHUNK, (c + 1) * FF_CHUNK)
        prev = lambda k, raw: jnp.where(pos < k, (p1_ref, p2_ref)[k - 1][:, cs], raw)
        gate_ref[:, cs] = _ffn_chunk(cs, hcat, hb, prev, wg_ref, wu_ref, cw_ref, cb_ref, gs_ref, act_ref)
    o_ref[...] = x + _dot(act_ref[...], wd_ref[...])


def _resident(a):
    zeros = (0,) * a.ndim
    return pl.BlockSpec(a.shape, lambda *_: zeros, pipeline_mode=pl.Buffered(1))


def _ffn_weights(g, w_gate, w_up, conv_w, conv_b, w_down):
    return (g.reshape(1, -1), w_gate.astype(BF16), w_up.astype(BF16), conv_w, conv_b.reshape(1, -1),
            w_down.astype(BF16))


def _ffn_prompt(x, g, w_gate, w_up, conv_w, conv_b, w_down, *, tm):
    B, T, D = x.shape
    dff = w_gate.shape[1]
    ws = _ffn_weights(g, w_gate, w_up, conv_w, conv_b, w_down)
    hpt = tm // HALO
    return pl.pallas_call(
        _ffn_prompt_kernel, grid=(B, T // tm),
        in_specs=[pl.BlockSpec((1, tm, D), lambda b, t: (b, t, 0)),
                  pl.BlockSpec((1, HALO, D), lambda b, t: (b, jnp.maximum(t * hpt - 1, 0), 0))]
                 + [_resident(w) for w in ws],
        out_specs=[pl.BlockSpec((1, tm, D), lambda b, t: (b, t, 0)),
                   pl.BlockSpec((1, CONV_W - 1, dff), lambda b, t: (b, 0, 0))],
        out_shape=[jax.ShapeDtypeStruct((B, T, D), F32), jax.ShapeDtypeStruct((B, CONV_W - 1, dff), F32)],
        scratch_shapes=[pltpu.VMEM((tm + HALO, FF_CHUNK), F32), pltpu.VMEM((tm, dff), BF16)],
        compiler_params=_cparams("parallel", "arbitrary"), name="ffn_prompt",
    )(x, x, *ws)


def _ffn_sample(x, state, g, w_gate, w_up, conv_w, conv_b, w_down):
    B, T, D = x.shape
    dff = w_gate.shape[1]
    ws = _ffn_weights(g, w_gate, w_up, conv_w, conv_b, w_down)
    zeros = jnp.zeros((B, T, dff), F32)
    p1 = zeros.at[:, 0].set(state[:, 1]).reshape(B * T, dff)
    p2 = zeros.at[:, 0].set(state[:, 0]).at[:, 1].set(state[:, 1]).reshape(B * T, dff)
    args = (x.reshape(B * T, D), p1, p2) + ws
    y, gate = pl.pallas_call(
        functools.partial(_ffn_sample_kernel, T), grid=(1,),
        in_specs=[_resident(a) for a in args],
        out_specs=[pl.BlockSpec((B * T, D), lambda i: (0, 0)), pl.BlockSpec((B * T, dff), lambda i: (0, 0))],
        out_shape=[jax.ShapeDtypeStruct((B * T, D), F32), jax.ShapeDtypeStruct((B * T, dff), F32)],
        scratch_shapes=[pltpu.VMEM((B * T + HALO, FF_CHUNK), F32), pltpu.VMEM((B * T, dff), BF16)],
        compiler_params=_cparams("arbitrary"), name="ffn_sample",
    )(*args)
    return y.reshape(B, T, D), gate.reshape(B, T, dff)[:, T - (CONV_W - 1):]


def _proj_diff_kernel(x_ref, g_ref, w_ref, qn_ref, kn_ref, gmat_ref, q_ref, k_ref, v_ref, kb_ref, vb_ref):
    wq = q_ref.shape[2]
    hb = _rmsnorm(x_ref[0], g_ref[...]).astype(BF16)
    gmat = gmat_ref[...]
    q_ref[0] = (_head_rmsnorm(_dot(hb, w_ref[:, :wq]), gmat, qn_ref[...]) * QK_SCALE).astype(BF16)
    k = _head_rmsnorm(_dot(hb, w_ref[:, wq:2 * wq]), gmat, kn_ref[...])
    k_ref[0] = k
    kb_ref[0] = k.astype(BF16)
    v = _dot(hb, w_ref[:, 2 * wq:])
    v_ref[0] = v
    vb_ref[0] = v.astype(BF16)


def _proj_diff(x, g, w_in, qn, kn, *, tm):
    B, T, D = x.shape
    wv = w_in.shape[1] // 3
    wq = wv
    reps = wq // HEAD_DIM
    args = (x, g.reshape(1, D), w_in.astype(BF16), jnp.tile(qn, reps).reshape(1, wq),
            jnp.tile(kn, reps).reshape(1, wq), _head_mean_matrix())
    full = lambda a: pl.BlockSpec(a.shape, lambda b, t: (0,) * a.ndim)
    row = lambda w: pl.BlockSpec((1, tm, w), lambda b, t: (b, t, 0))
    sds = lambda w, dt: jax.ShapeDtypeStruct((B, T, w), dt)
    return pl.pallas_call(
        _proj_diff_kernel, grid=(B, T // tm),
        in_specs=[row(D)] + [full(a) for a in args[1:]],
        out_specs=[row(wq), row(wq), row(wv), row(wq), row(wv)],
        out_shape=[sds(wq, BF16), sds(wq, F32), sds(wv, F32), sds(wq, BF16), sds(wv, BF16)],
        compiler_params=_cparams("parallel", "parallel"), name="proj_diff",
    )(*args)


def _t5_bucket(rel):
    rel = np.asarray(rel)
    max_exact = N_BUCKETS // 2
    relf = np.maximum(rel, 1).astype(np.float32)
    large = max_exact + (np.log(relf / np.float32(max_exact)) / np.float32(math.log(MAX_DISTANCE / max_exact))
                         * np.float32(N_BUCKETS - max_exact)).astype(np.int32)
    large = np.minimum(large, N_BUCKETS - 1)
    return np.where(rel < max_exact, rel, large).astype(np.int32)


def _near_buckets(q_pos):
    r = np.asarray(q_pos)[:, None]
    c = np.arange(BLK)[None, :]
    assert _t5_bucket(2 * BLK - (BLK - 1)) == N_BUCKETS - 1
    return jnp.asarray(np.stack([_t5_bucket(np.maximum(r - c, 0)), _t5_bucket(BLK + r - c)]))


def _diff_lambda(lq1_ref, lk1_ref, lq2_ref, lk2_ref, lam_init):
    s1 = jnp.sum(lq1_ref[...] * lk1_ref[...], axis=1, keepdims=True)
    s2 = jnp.sum(lq2_ref[...] * lk2_ref[...], axis=1, keepdims=True)
    return jnp.exp(s1) - jnp.exp(s2) + lam_init


def _diff_prompt_kernel(lam_init, rb_ref, q_ref, k_ref, v_ref, bk_ref, lq1_ref, lk1_ref, lq2_ref, lk2_ref,
                        sub_ref, o_ref, m_ref, l_ref, acc_ref, bias_ref, kbd_ref, vbd_ref):
    T = q_ref.shape[1]
    tb = bias_ref.shape[1]
    nsub = tb // BLK
    h = pl.program_id(1)
    lo16 = _head0_mask(BF16, tb)
    lam = _diff_lambda(lq1_ref, lk1_ref, lq2_ref, lk2_ref, lam_init)
    far_bias = rb_ref[N_BUCKETS - 1, h]

    near = []
    for d in range(2):
        tile_bias = jnp.zeros((BLK, BLK), F32)
        for b in range(N_BUCKETS):
            tile_bias = jnp.where(bk_ref[d] == b, rb_ref[b, h], tile_bias)
        near.append(tile_bias)
    bias_ref[...] = jnp.full(bias_ref.shape, far_bias, F32)
    for a in range(nsub):
        bias_ref[0, a * BLK:(a + 1) * BLK, a * BLK:(a + 1) * BLK] = near[0]
        if a >= 1:
            bias_ref[0, a * BLK:(a + 1) * BLK, (a - 1) * BLK:a * BLK] = near[1]
    bias_ref[1, 0:BLK, (nsub - 1) * BLK:nsub * BLK] = near[1]

    def prepare(j, carry):
        ks = pl.ds(pl.multiple_of(j * tb, tb), tb)
        kbd_ref[j] = _block_diag_pair(k_ref[0, ks, :], lo16)
        vb = v_ref[0, ks, :]
        zero = jnp.zeros_like(vb)
        vbd_ref[j] = jnp.concatenate([jnp.concatenate([vb, zero], axis=1), jnp.concatenate([zero, vb], axis=1)], axis=0)
        return carry

    lax.fori_loop(0, T // tb, prepare, 0)

    def tile(q2, j, bias, masked):
        z = _dot_nt(q2, kbd_ref[j])
        ps, alphas = [], []
        for mp in range(2):
            s = z[:, mp * tb:(mp + 1) * tb] + bias
            if masked:
                s = jnp.where(_causal_mask(tb), s, NEG)
            p, alpha = _softmax_step(s, m_ref, l_ref, mp)
            ps.append(p.astype(BF16))
            alphas.append(alpha)
        pv = _dot(jnp.concatenate(ps, axis=1), vbd_ref[j])
        acc_ref[...] = jnp.concatenate(alphas, axis=1) * acc_ref[...] + pv

    def qblock(i, carry):
        qs = pl.ds(pl.multiple_of(i * tb, tb), tb)
        q2 = q_ref[0, qs, :]
        m_ref[...] = jnp.full_like(m_ref, NEG)
        l_ref[...] = jnp.zeros_like(l_ref)
        acc_ref[...] = jnp.zeros_like(acc_ref)
        tile(q2, i, bias_ref[0], True)

        @pl.when(i >= 1)
        def _():
            tile(q2, i - 1, bias_ref[1], False)

        lax.fori_loop(0, jnp.maximum(i - 1, 0), lambda j, c: (tile(q2, j, far_bias, False), c)[1], 0)
        o_map = [acc_ref[:, mp * LANES:(mp + 1) * LANES] / jnp.sum(l_ref[mp], axis=1, keepdims=True)
                 for mp in range(2)]
        o = o_map[0] - lam * o_map[1]
        o_ref[0, qs, :] = (_rmsnorm(o, sub_ref[...]) * (1.0 - lam_init)).astype(o_ref.dtype)
        return carry

    lax.fori_loop(0, T // tb, qblock, 0)


def _diff_prompt(q, kb, vb, rel_bias, lq1, lk1, lq2, lk2, subln, lam_init, *, tb):
    B, T, W = q.shape
    n_heads = W // LANES
    buckets = _near_buckets(np.arange(BLK))
    spec = pl.BlockSpec((1, T, LANES), lambda b, h: (b, 0, h))
    small = lambda a: pl.BlockSpec(a.shape, lambda b, h: (0,) * a.ndim)
    vecs = [a.reshape(1, -1) for a in (lq1, lk1, lq2, lk2, subln)]
    return pl.pallas_call(
        functools.partial(_diff_prompt_kernel, lam_init), grid=(B, n_heads),
        in_specs=[pl.BlockSpec(memory_space=pltpu.SMEM)] + [spec] * 3 + [small(buckets)] + [small(a) for a in vecs],
        out_specs=spec,
        scratch_shapes=[pltpu.VMEM((2, tb, LANES), F32), pltpu.VMEM((2, tb, LANES), F32),
                        pltpu.VMEM((tb, 2 * LANES), F32), pltpu.VMEM((2, tb, tb), F32),
                        pltpu.VMEM((T // tb, 2 * tb, LANES), BF16), pltpu.VMEM((T // tb, 2 * tb, 2 * LANES), BF16)],
        out_shape=jax.ShapeDtypeStruct((B, T, W), BF16),
        compiler_params=_cparams("parallel", "parallel"), name="diff_prompt",
    )(rel_bias, q, kb, vb, buckets, *vecs)


GROUPS_PER_CHUNK = MXU_DIM // HEAD_DIM


def _chunked_queries(q):
    B, T, W = q.shape
    qg = q.reshape(B, T, W // MXU_DIM, GROUPS_PER_CHUNK, HEAD_DIM).transpose(0, 2, 3, 1, 4)
    eye = jnp.eye(GROUPS_PER_CHUNK, dtype=q.dtype)
    out = qg[:, :, :, :, None, :] * eye[None, None, :, None, :, None]
    return out.reshape(B, W // MXU_DIM, GROUPS_PER_CHUNK * T, MXU_DIM)


def _keys_on_lanes(cache):
    nd = cache.ndim
    t = cache.transpose((0, 1) + tuple(range(3, nd)) + (2,))
    return t.reshape(cache.shape[0], cache.shape[1], -1, cache.shape[2])


def _scores(qc_ref, key_tiles, transposed):
    rows = []
    for c in range(qc_ref.shape[1]):
        cs = slice(c * MXU_DIM, (c + 1) * MXU_DIM)
        if transposed:
            rows.append(_dot(qc_ref[0, c], jnp.concatenate([kt[cs, :].astype(BF16) for kt in key_tiles], axis=1)))
        else:
            rows.append(_dot_nt(qc_ref[0, c], jnp.concatenate([kt[:, cs].astype(BF16) for kt in key_tiles], axis=0)))
    return jnp.concatenate(rows, axis=0)


def _mix_sample_kernel(seq, n_sb, n_pages, pt_ref, qc_ref, kn_ref, vn_ref, lfn_ref, *refs):
    kp = refs[:n_pages]
    vp = refs[n_pages:2 * n_pages]
    lfp = refs[2 * n_pages:3 * n_pages]
    o_ref, run_ref, m_ref, l_ref, acc_ref = refs[3 * n_pages:]
    j = pl.program_id(1)
    sb = n_sb * seq
    n_fox = (BLK - sb) // seq
    rpc = GROUPS_PER_CHUNK * seq
    n_chunks = qc_ref.shape[1]
    suffix = _suffix_matrix()

    def block(z, lfts, pv_of, mask):
        n = len(lfts)
        zs, zf = z[:sb], z[sb:]
        ls = _log_sigmoid(zs)
        l1m = ls - zs
        a_parts, s_parts = [], []
        for g in range(n):
            gs = slice(g * BLK, (g + 1) * BLK)
            elf = jnp.concatenate([jnp.broadcast_to(lfts[g][h:h + 1, :], (seq, BLK)) for h in range(n_fox)], axis=0)
            x = jnp.concatenate([l1m[:, gs], elf], axis=0)
            if mask is not None:
                x = jnp.where(mask, x, 0.0)
            cs = _dot(_split(x), suffix)
            run = run_ref[...]
            run_ref[...] = run + cs[:, BLK:]
            later = cs[:, :BLK] + run
            a = jnp.exp(ls[:, gs] + later[:sb])
            s = zf[:, gs] + later[sb:]
            if mask is not None:
                a = jnp.where(mask[:sb], a, 0.0)
                s = jnp.where(mask[sb:], s, NEG)
            a_parts.append(a)
            s_parts.append(s)
        p, alpha = _softmax_step(jnp.concatenate(s_parts, axis=1), m_ref, l_ref, 0)
        w = jnp.concatenate([jnp.concatenate(a_parts, axis=1), p], axis=0).astype(BF16)
        for c in range(n_chunks):
            pv = pv_of(c, w[c * rpc:(c + 1) * rpc])
            if (c + 1) * rpc <= sb:
                acc_ref[c] += pv
            else:
                al = alpha[c * rpc - sb:(c + 1) * rpc - sb]
                acc_ref[c] = jnp.concatenate([al] * (MXU_DIM // LANES), axis=1) * acc_ref[c] + pv

    @pl.when(j == 0)
    def _():
        run_ref[...] = jnp.zeros_like(run_ref)
        m_ref[...] = jnp.full_like(m_ref, NEG)
        l_ref[...] = jnp.zeros_like(l_ref)
        acc_ref[...] = jnp.zeros_like(acc_ref)
        lane = lax.broadcasted_iota(jnp.int32, (BLK, BLK), 1)
        row = lax.broadcasted_iota(jnp.int32, (BLK, BLK), 0)
        own = lane < row % seq + jnp.where(row < sb, 0, 1)
        kn = _pad_rows(kn_ref[0], BLK)
        vn = _pad_rows(vn_ref[0], BLK).astype(BF16)
        block(_scores(qc_ref, [kn], False), [lfn_ref[0]],
              lambda c, w: _dot(w, vn[:, c * MXU_DIM:(c + 1) * MXU_DIM]), own)

    def page_values(c, w):
        cs = slice(c * MXU_DIM, (c + 1) * MXU_DIM)
        return _dot_nt(w, jnp.concatenate([v[cs, :].astype(BF16) for v in vp], axis=1))

    block(_scores(qc_ref, kp, True), [r[...] for r in lfp], page_values, None)

    @pl.when(j == pl.num_programs(1) - 1)
    def _():
        col_group = lax.broadcasted_iota(jnp.int32, (seq, MXU_DIM), 1) // HEAD_DIM
        denom = jnp.sum(l_ref[0], axis=1, keepdims=True)
        outs = []
        for c in range(n_chunks):
            rows = acc_ref[c]
            if (c + 1) * rpc > sb:
                rows = rows / denom[c * rpc - sb:(c + 1) * rpc - sb]
            out = jnp.zeros((seq, MXU_DIM), F32)
            for g in range(GROUPS_PER_CHUNK):
                out = jnp.where(col_group == g, rows[g * seq:(g + 1) * seq, :], out)
            outs.append(out)
        o_ref[0] = jnp.concatenate(outs, axis=1)


def _paged_specs(shape_tail, layer, n_total, per_step):
    def spec(g):
        zeros = (0,) * len(shape_tail)
        return pl.BlockSpec((None, None) + shape_tail,
                            lambda b, j, pt: (layer, pt[b, n_total - 1 - (j * per_step + g)]) + zeros)
    return [spec(g) for g in range(per_step)]


def _mix_sample(q, k_new, v_new, logf_new, cache_k, cache_v, cache_logf, layer, page_table, n_sb):
    B, T, W = q.shape
    n_heads = W // HEAD_DIM
    n_fox = n_heads - n_sb
    n_total = page_table.shape[1]
    g = min(PAGES_PER_STEP, n_total)
    assert n_heads * T == BLK and cache_k.shape[2] == BLK and n_total % g == 0
    assert (n_sb * T) % (GROUPS_PER_CHUNK * T) == 0
    lfn = jnp.zeros((B, n_fox, BLK), F32).at[:, :, :T].set(logf_new.transpose(0, 2, 1))
    args = (_chunked_queries(q), k_new, v_new, lfn)
    per_seq = lambda a: pl.BlockSpec((1,) + a.shape[1:], lambda b, j, pt: (b,) + (0,) * (a.ndim - 1))
    return pl.pallas_call(
        functools.partial(_mix_sample_kernel, T, n_sb, g),
        grid_spec=pltpu.PrefetchScalarGridSpec(
            num_scalar_prefetch=1, grid=(B, n_total // g),
            in_specs=[per_seq(a) for a in args] + _paged_specs((W, BLK), layer, n_total, g) * 2
                     + _paged_specs((n_fox, BLK), layer, n_total, g),
            out_specs=pl.BlockSpec((1, T, W), lambda b, j, pt: (b, 0, 0)),
            scratch_shapes=[pltpu.VMEM((BLK, BLK), F32), pltpu.VMEM((1, n_fox * T, BLK), F32),
                            pltpu.VMEM((1, n_fox * T, BLK), F32),
                            pltpu.VMEM((W // MXU_DIM, GROUPS_PER_CHUNK * T, MXU_DIM), F32)]),
        out_shape=jax.ShapeDtypeStruct((B, T, W), F32),
        compiler_params=_cparams("parallel", "arbitrary"), name="mix_sample",
    )(page_table, *args, *([_keys_on_lanes(cache_k)] * g), *([_keys_on_lanes(cache_v)] * g),
      *([_keys_on_lanes(cache_logf)] * g))


def _diff_sample_kernel(seq, lam_init, n_pages, pt_ref, qc_ref, kn_ref, vn_ref, bk_ref, rbr_ref,
                        lq1_ref, lk1_ref, lq2_ref, lk2_ref, sub_ref, *refs):
    kp = refs[:n_pages]
    vp = refs[n_pages:2 * n_pages]
    o_ref, m_ref, l_ref, acc_ref, bias_ref = refs[2 * n_pages:]
    j = pl.program_id(1)
    n_heads = vn_ref.shape[2] // BLK
    rph = 2 * seq

    def near_bias(d):
        out = jnp.zeros((BLK, BLK), F32)
        for b in range(N_BUCKETS):
            out = jnp.where(bk_ref[d] == b, rbr_ref[:, b:b + 1], out)
        return out

    def block(s, values_of, mask):
        if mask is not None:
            s = jnp.where(mask, s, NEG)
        p, alpha = _softmax_step(s, m_ref, l_ref, 0)
        w = p.astype(BF16)
        pv = jnp.concatenate([_dot(w[h * rph:(h + 1) * rph], values_of(h)) for h in range(n_heads)], axis=0)
        acc_ref[...] = alpha * acc_ref[...] + pv

    far = jnp.broadcast_to(rbr_ref[:, N_BUCKETS - 1:N_BUCKETS], (BLK, BLK))

    @pl.when(j == 0)
    def _():
        m_ref[...] = jnp.full_like(m_ref, NEG)
        l_ref[...] = jnp.zeros_like(l_ref)
        acc_ref[...] = jnp.zeros_like(acc_ref)
        lane = lax.broadcasted_iota(jnp.int32, (BLK, BLK), 1)
        row = lax.broadcasted_iota(jnp.int32, (BLK, BLK), 0)
        kn = _pad_rows(kn_ref[0], BLK)
        block(_scores(qc_ref, [kn], False) + near_bias(0),
              lambda h: _pad_rows(vn_ref[0, :, h * BLK:(h + 1) * BLK], BLK).astype(BF16), lane <= row % seq)
        bias_ref[...] = jnp.concatenate([near_bias(1)] + [far] * (n_pages - 1), axis=1)

    @pl.when(j == 1)
    def _():
        bias_ref[...] = jnp.concatenate([far] * n_pages, axis=1)

    block(_scores(qc_ref, kp, True) + bias_ref[...],
          lambda h: jnp.concatenate([v[pl.ds(h, BLK, stride=n_heads), :].astype(BF16) for v in vp], axis=0), None)

    @pl.when(j == pl.num_programs(1) - 1)
    def _():
        rows = acc_ref[...] / jnp.sum(l_ref[0], axis=1, keepdims=True)
        lam = _diff_lambda(lq1_ref, lk1_ref, lq2_ref, lk2_ref, lam_init)
        sub = sub_ref[...]
        outs = []
        for h in range(n_heads):
            o = rows[h * rph:h * rph + seq] - lam * rows[h * rph + seq:(h + 1) * rph]
            outs.append(_rmsnorm(o, sub))
        o_ref[0] = jnp.concatenate(outs, axis=1) * (1.0 - lam_init)


def _diff_sample(q, k_new, v_new, cache_k, cache_v, layer, page_table, rel_bias, lq1, lk1, lq2, lk2, subln, lam_init):
    B, T, W = q.shape
    n_heads = cache_v.shape[3]
    n_total = page_table.shape[1]
    g = min(PAGES_PER_STEP, n_total)
    assert (W // HEAD_DIM) * T == BLK and cache_k.shape[2] == BLK and n_total % g == 0
    assert cache_v.shape[4] == BLK and v_new.shape[2] == n_heads * BLK
    L, P = cache_v.shape[:2]
    v_rows = cache_v.reshape(L, P, BLK * n_heads, BLK)
    buckets = _near_buckets(np.arange(BLK) % T)
    rb_rows = jnp.repeat(rel_bias.T, 2 * T, axis=0)
    vecs = [a.reshape(1, -1) for a in (lq1, lk1, lq2, lk2, subln)]
    args = (_chunked_queries(q), k_new, v_new)
    per_seq = lambda a: pl.BlockSpec((1,) + a.shape[1:], lambda b, j, pt: (b,) + (0,) * (a.ndim - 1))
    small = lambda a: pl.BlockSpec(a.shape, lambda b, j, pt: (0,) * a.ndim)
    return pl.pallas_call(
        functools.partial(_diff_sample_kernel, T, lam_init, g),
        grid_spec=pltpu.PrefetchScalarGridSpec(
            num_scalar_prefetch=1, grid=(B, n_total // g),
            in_specs=[per_seq(a) for a in args] + [small(buckets), small(rb_rows)] + [small(a) for a in vecs]
                     + _paged_specs((W, BLK), layer, n_total, g) + _paged_specs((BLK * n_heads, BLK), layer, n_total, g),
            out_specs=pl.BlockSpec((1, T, n_heads * BLK), lambda b, j, pt: (b, 0, 0)),
            scratch_shapes=[pltpu.VMEM((1, BLK, BLK), F32), pltpu.VMEM((1, BLK, BLK), F32),
                            pltpu.VMEM((BLK, BLK), F32), pltpu.VMEM((BLK, g * BLK), F32)]),
        out_shape=jax.ShapeDtypeStruct((B, T, n_heads * BLK), F32),
        compiler_params=_cparams("parallel", "arbitrary"), name="diff_sample",
    )(page_table, *args, buckets, rb_rows, *vecs, *([_keys_on_lanes(cache_k)] * g), *([v_rows] * g))


def kernel(x_prompt, x_sample, cache_k_mix, cache_v_mix, cache_logf_mix, cache_k_diff, cache_v_diff, state_ffn_conv, page_table, rel_bias, norm_mix, w_in_mix, b_forget, qnorm_fox, knorm_fox, w_out_mix, norm_diff, w_in_diff, qnorm_diff, knorm_diff, lambda_q1, lambda_k1, lambda_q2, lambda_k2, subln_diff, w_out_diff, norm_ffn, w_gate, w_up, conv_w, conv_b, w_down):
    B, T, D = x_prompt.shape
    DB, DT, _ = x_sample.shape
    depth = norm_ffn.shape[0]
    n_fox = b_forget.shape[1]
    n_mix = cache_k_mix.shape[3]
    n_sb = n_mix - n_fox
    n_diff = cache_k_diff.shape[3]
    sb_pairs = n_sb * HEAD_DIM // LANES
    fox_pairs = n_fox * HEAD_DIM // LANES
    tm = min(T, PROMPT_ROW_TILE)
    tb = min(T, ATT_BLOCK)
    xp, xs = x_prompt, x_sample
    outs = {n: [] for n in ("kmp", "vmp", "lfp", "kms", "vms", "lfs", "kdp", "vdp", "kds", "vds", "cp", "cs")}
    for l in range(depth):
        j = l // 2
        if l % 2 == 0:
            pw = (norm_mix[j], w_in_mix[j], b_forget[j], qnorm_fox[j], knorm_fox[j])
            q, k, v, kb, vb, logf, c, ct = _proj_mix(xp, *pw, tm=tm, with_cumsum=True)
            o_sb = _sb_prompt(q, kb, vb, sb_pairs)
            o_fx = _fox_prompt(q, kb, vb, c, ct, sb_pairs, fox_pairs, tb=tb)
            xp = _out_proj(xp.reshape(B * T, D), [o_sb.reshape(B * T, -1), o_fx.reshape(B * T, -1)], w_out_mix[j],
                           tm=tm).reshape(B, T, D)
            outs["kmp"].append(k.reshape(B, T, n_mix, HEAD_DIM))
            outs["vmp"].append(v.reshape(B, T, n_mix, HEAD_DIM))
            outs["lfp"].append(logf)
            qs, ks, vs, _, _, lfs = _proj_mix(xs.reshape(1, DB * DT, D), *pw, tm=DB * DT, with_cumsum=False)
            ks, vs, lfs = ks.reshape(DB, DT, -1), vs.reshape(DB, DT, -1), lfs.reshape(DB, DT, n_fox)
            o = _mix_sample(qs.reshape(DB, DT, -1), ks, vs, lfs, cache_k_mix, cache_v_mix, cache_logf_mix, j,
                            page_table, n_sb)
            xs = _out_proj(xs.reshape(DB * DT, D), [o.reshape(DB * DT, -1).astype(BF16)], w_out_mix[j],
                           tm=DB * DT).reshape(DB, DT, D)
            outs["kms"].append(ks.reshape(DB, DT, n_mix, HEAD_DIM))
            outs["vms"].append(vs.reshape(DB, DT, n_mix, HEAD_DIM))
            outs["lfs"].append(lfs)
        else:
            lam_init = 0.8 - 0.6 * math.exp(-0.3 * l)
            pw = (norm_diff[j], w_in_diff[j], qnorm_diff[j], knorm_diff[j])
            lam_w = (lambda_q1[j], lambda_k1[j], lambda_q2[j], lambda_k2[j], subln_diff[j], lam_init)
            q, k, v, kb, vb = _proj_diff(xp, *pw, tm=tm)
            o = _diff_prompt(q, kb, vb, rel_bias, *lam_w, tb=tb)
            xp = _out_proj(xp.reshape(B * T, D), [o.reshape(B * T, -1)], w_out_diff[j], tm=tm).reshape(B, T, D)
            outs["kdp"].append(k.reshape(B, T, n_diff, 2, HEAD_DIM))
            outs["vdp"].append(v.reshape(B, T, n_diff, 2 * HEAD_DIM))
            qs, ks, vs, _, _ = _proj_diff(xs.reshape(1, DB * DT, D), *pw, tm=DB * DT)
            ks, vs = ks.reshape(DB, DT, -1), vs.reshape(DB, DT, -1)
            o = _diff_sample(qs.reshape(DB, DT, -1), ks, vs, cache_k_diff, cache_v_diff, j, page_table, rel_bias,
                             *lam_w)
            xs = _out_proj(xs.reshape(DB * DT, D), [o.reshape(DB * DT, -1).astype(BF16)], w_out_diff[j],
                           tm=DB * DT).reshape(DB, DT, D)
            outs["kds"].append(ks.reshape(DB, DT, n_diff, 2, HEAD_DIM))
            outs["vds"].append(vs.reshape(DB, DT, n_diff, 2 * HEAD_DIM))
        fw = (norm_ffn[l], w_gate[l], w_up[l], conv_w[l], conv_b[l], w_down[l])
        xp, cp = _ffn_prompt(xp, *fw, tm=tm)
        xs, cs = _ffn_sample(xs, state_ffn_conv[l], *fw)
        outs["cp"].append(cp)
        outs["cs"].append(cs)
    st = {n: jnp.stack(a) for n, a in outs.items()}
    return (xp, xs, st["kmp"], st["vmp"], st["lfp"], st["kms"], st["vms"], st["lfs"], st["kdp"], st["vdp"],
            st["kds"], st["vds"], st["cp"], st["cs"])
```

```python
import functools
import math

import numpy as np
import jax
import jax.numpy as jnp
from jax import lax
from jax.experimental import pallas as pl
from jax.experimental.pallas import tpu as pltpu

F32 = jnp.float32
BF16 = jnp.bfloat16

HEAD_DIM = 64
N_BUCKETS = 32
MAX_DISTANCE = 128
CONV_W = 3
EPS = 1e-6
QK_SCALE = HEAD_DIM ** -0.5

LANES = 128
SUBLANES = 8
MXU_DIM = 256
VMEM_LIMIT = 56 * 1024 * 1024

BLK = LANES
NEG = -1e30
EXP_ZERO_BELOW = -104.0

PROMPT_ROW_TILE = 512
FFN_ROW_TILE = 1024
ATT_BLOCK = 512
ATT_UNITS = 2
SB_GROUP = 4
PAGES_PER_STEP = 8


def _cparams(*sem):
    return pltpu.CompilerParams(dimension_semantics=sem, vmem_limit_bytes=VMEM_LIMIT)


def _dot(a, b):
    return jnp.dot(a, b, preferred_element_type=F32)


def _dot_nt(a, b):
    return lax.dot_general(a, b, (((1,), (1,)), ((), ())), preferred_element_type=F32)


def _dot_f32(a, b):
    return jnp.dot(a, b, preferred_element_type=F32, precision=lax.Precision.HIGHEST)


def _split(a):
    hi = a.astype(BF16)
    lo = (a - hi.astype(F32)).astype(BF16)
    return jnp.concatenate([hi, lo], axis=1)


def _log_sigmoid(x):
    return jnp.minimum(x, 0.0) - jnp.log1p(jnp.exp(-jnp.abs(x)))


def _log_sigmoid_scores(x):
    return jnp.minimum(x, 0.0) - jnp.log(1.0 + jnp.exp(-jnp.abs(x)))


def _rmsnorm(x, g):
    return x * lax.rsqrt(jnp.mean(x * x, axis=-1, keepdims=True) + EPS) * g


def _head_rmsnorm(y, gmat2, gain):
    outs = []
    for c in range(y.shape[1] // LANES):
        yc = y[:, c * LANES:(c + 1) * LANES]
        outs.append(yc * lax.rsqrt(_dot(_split(yc * yc), gmat2) + EPS))
    return jnp.concatenate(outs, axis=1) * gain


def _head_mean_matrix():
    idx = np.arange(LANES) // HEAD_DIM
    g = (idx[:, None] == idx[None, :]).astype(np.float32) / HEAD_DIM
    return jnp.asarray(np.concatenate([g, g], axis=0), BF16)


def _suffix_matrix():
    r = lax.broadcasted_iota(jnp.int32, (2 * BLK, 2 * BLK), 0) & (BLK - 1)
    c = lax.broadcasted_iota(jnp.int32, (2 * BLK, 2 * BLK), 1)
    return jnp.where((c >= BLK) | (r > c), 1.0, 0.0).astype(BF16)


def _head0_mask(dtype, rows=BLK):
    lane = lax.broadcasted_iota(jnp.int32, (rows, LANES), 1)
    if dtype == F32:
        return lane < HEAD_DIM
    return jnp.where(lane < HEAD_DIM, 1.0, 0.0).astype(dtype) > 0


def _block_diag_pair(x, lo_half):
    zero = jnp.zeros_like(x)
    return jnp.concatenate([jnp.where(lo_half, x, zero), jnp.where(lo_half, zero, x)], axis=0)


def _pad_rows(x, n):
    return jnp.concatenate([x, jnp.zeros((n - x.shape[0],) + x.shape[1:], x.dtype)], axis=0)


def _proj_mix_kernel(with_cumsum, x_ref, g_ref, w_ref, wf_ref, wft_ref, bfr_ref, bfc_ref, qn_ref, kn_ref,
                     gmat_ref, q_ref, k_ref, v_ref, kb_ref, vb_ref, logf_ref, *rest):
    ws = q_ref.shape[2] // 2
    n_fox = logf_ref.shape[2]
    tm = x_ref.shape[1]
    hb = _rmsnorm(x_ref[0], g_ref[...]).astype(BF16)

    def sec(i):
        return _dot(hb, w_ref[:, i * ws:(i + 1) * ws])

    q_ref[0, :, :ws] = (sec(0) * QK_SCALE).astype(BF16)
    k_sb = sec(1)
    k_ref[0, :, :ws] = k_sb
    kb_ref[0, :, :ws] = k_sb.astype(BF16)
    v_sb = sec(2)
    v_ref[0, :, :ws] = v_sb
    vb_ref[0, :, :ws] = v_sb.astype(BF16)
    gmat = gmat_ref[...]
    q_ref[0, :, ws:] = (_head_rmsnorm(sec(3), gmat, qn_ref[...]) * QK_SCALE).astype(BF16)
    k_fx = _head_rmsnorm(sec(4), gmat, kn_ref[...])
    k_ref[0, :, ws:] = k_fx
    kb_ref[0, :, ws:] = k_fx.astype(BF16)
    v_fx = sec(5)
    v_ref[0, :, ws:] = v_fx
    vb_ref[0, :, ws:] = v_fx.astype(BF16)

    logf = _log_sigmoid(_dot(hb, wf_ref[...]) + bfr_ref[...])
    logf_ref[0] = logf[:, :n_fox]
    if not with_cumsum:
        return
    c_ref, ct_ref, carry_ref, carryt_ref = rest

    @pl.when(pl.program_id(1) == 0)
    def _():
        carry_ref[...] = jnp.zeros_like(carry_ref)
        carryt_ref[...] = jnp.zeros_like(carryt_ref)

    r = lax.broadcasted_iota(jnp.int32, (tm, tm), 0)
    c = lax.broadcasted_iota(jnp.int32, (tm, tm), 1)
    csum = _dot_f32(jnp.where(c <= r, 1.0, 0.0).astype(F32), logf) + carry_ref[...]
    c_ref[0] = csum[:, :n_fox]
    carry_ref[...] = csum[tm - 1:tm, :]
    logft = _log_sigmoid(_dot_nt(wft_ref[...], hb) + bfc_ref[:, 0:1])
    csumt = _dot_f32(logft, jnp.where(r <= c, 1.0, 0.0).astype(F32)) + carryt_ref[:, 0:1]
    ct_ref[0] = csumt[:n_fox, :]
    carryt_ref[...] = jnp.broadcast_to(csumt[:, tm - 1:tm], carryt_ref.shape)


def _proj_mix(x, g, w_in, b_f, qn, kn, *, tm, with_cumsum):
    B, T, D = x.shape
    n_fox = b_f.shape[0]
    ws = (w_in.shape[1] - n_fox) // 6
    wmain = w_in[:, :6 * ws].astype(BF16)
    wf = jnp.zeros((D, LANES), BF16).at[:, :n_fox].set(w_in[:, 6 * ws:].astype(BF16))
    wft = jnp.zeros((2 * SUBLANES, D), BF16).at[:n_fox, :].set(w_in[:, 6 * ws:].T.astype(BF16))
    bfr = jnp.zeros((1, LANES), F32).at[0, :n_fox].set(b_f)
    bfc = jnp.zeros((2 * SUBLANES, LANES), F32).at[:n_fox, :].set(jnp.broadcast_to(b_f[:, None], (n_fox, LANES)))
    reps = ws // HEAD_DIM
    full = lambda a: pl.BlockSpec(a.shape, lambda b, t: (0,) * a.ndim)
    row = lambda w: pl.BlockSpec((1, tm, w), lambda b, t: (b, t, 0))
    args = (x, g.reshape(1, D), wmain, wf, wft, bfr, bfc, jnp.tile(qn, reps).reshape(1, ws),
            jnp.tile(kn, reps).reshape(1, ws), _head_mean_matrix())
    out_shape = [jax.ShapeDtypeStruct((B, T, 2 * ws), BF16), jax.ShapeDtypeStruct((B, T, 2 * ws), F32),
                 jax.ShapeDtypeStruct((B, T, 2 * ws), F32), jax.ShapeDtypeStruct((B, T, 2 * ws), BF16),
                 jax.ShapeDtypeStruct((B, T, 2 * ws), BF16), jax.ShapeDtypeStruct((B, T, n_fox), F32)]
    out_specs = [row(2 * ws)] * 5 + [row(n_fox)]
    scratch = []
    if with_cumsum:
        out_shape += [jax.ShapeDtypeStruct((B, T, n_fox), F32), jax.ShapeDtypeStruct((B, n_fox, T), F32)]
        out_specs += [row(n_fox), pl.BlockSpec((1, n_fox, tm), lambda b, t: (b, 0, t))]
        scratch = [pltpu.VMEM((1, LANES), F32), pltpu.VMEM((2 * SUBLANES, LANES), F32)]
    return pl.pallas_call(
        functools.partial(_proj_mix_kernel, with_cumsum),
        grid=(B, T // tm),
        in_specs=[row(D)] + [full(a) for a in args[1:]],
        out_specs=out_specs, out_shape=out_shape, scratch_shapes=scratch,
        compiler_params=_cparams("parallel", "arbitrary"), name="proj_mix",
    )(*args)


def _sb_prompt_kernel(q_ref, k_ref, v_ref, o_ref, r_ref, acc_ref):
    T = q_ref.shape[1]
    group = r_ref.shape[0]
    lane = lax.broadcasted_iota(jnp.int32, (BLK, BLK), 1)
    row = lax.broadcasted_iota(jnp.int32, (BLK, BLK), 0)
    lo_half = _head0_mask(BF16)
    strict = lane < row
    strict2 = jnp.concatenate([strict, strict], axis=1)
    suffix = _suffix_matrix()

    def tile(a, iq, ik, valid, masked):
        q2 = q_ref[0, pl.ds(pl.multiple_of(iq * BLK, BLK), BLK), :]
        ks = pl.ds(pl.multiple_of(ik * BLK, BLK), BLK)
        z = _dot_nt(q2, _block_diag_pair(k_ref[0, ks, :], lo_half))
        ls = _log_sigmoid_scores(z)
        l1m = ls - z
        if masked:
            l1m = jnp.where(strict2, l1m, 0.0)
        w = []
        for h in range(2):
            hs = slice(h * BLK, (h + 1) * BLK)
            cs = _dot(_split(l1m[:, hs]), suffix)
            run = r_ref[a, :, hs]
            wh = jnp.exp(ls[:, hs] + cs[:, :BLK] + run)
            total = cs[:, BLK:]
            if masked:
                wh = jnp.where(strict, wh, 0.0)
            if valid is not None:
                wh = wh * valid
                total = total * valid
            w.append(wh.astype(BF16))
            r_ref[a, :, hs] = run + total
        acc_ref[a] += _dot(jnp.concatenate(w, axis=1), _block_diag_pair(v_ref[0, ks, :], lo_half))

    def qgroup(i, carry):
        base = i * group
        r_ref[...] = jnp.zeros_like(r_ref)
        acc_ref[...] = jnp.zeros_like(acc_ref)
        for a in range(group):
            tile(a, base + a, base + a, None, True)

        def alive():
            return (jnp.max(r_ref[...]) > EXP_ZERO_BELOW).astype(jnp.int32)

        def cond(s):
            return (s[0] < base + group) & (s[1] > 0)

        def body(s):
            d = s[0]
            for a in range(group):
                ik = base + a - d
                tile(a, base + a, jnp.maximum(ik, 0), (ik >= 0).astype(F32), False)
            return d + 1, alive()

        lax.while_loop(cond, body, (1, alive()))
        for a in range(group):
            o_ref[0, pl.ds(pl.multiple_of((base + a) * BLK, BLK), BLK), :] = acc_ref[a].astype(o_ref.dtype)
        return carry

    lax.fori_loop(0, T // (group * BLK), qgroup, 0)


def _sb_prompt(q, kb, vb, n_pairs, *, group):
    B, T, _ = q.shape
    spec = pl.BlockSpec((1, T, LANES), lambda b, p: (b, 0, p))
    return pl.pallas_call(
        _sb_prompt_kernel, grid=(B, n_pairs), in_specs=[spec] * 3, out_specs=spec,
        out_shape=jax.ShapeDtypeStruct((B, T, n_pairs * LANES), BF16),
        scratch_shapes=[pltpu.VMEM((group, BLK, 2 * BLK), F32), pltpu.VMEM((group, BLK, BLK), F32)],
        compiler_params=_cparams("parallel", "parallel"), name="sb_prompt",
    )(q, kb, vb)


def _softmax_step(s, m_ref, l_ref, idx, shift=None):
    n_chunks = s.shape[1] // LANES
    chunk_max = s[:, :LANES]
    for c in range(1, n_chunks):
        chunk_max = jnp.maximum(chunk_max, s[:, c * LANES:(c + 1) * LANES])
    m_old = m_ref[idx]
    row_max = jnp.max(chunk_max, axis=1, keepdims=True)
    m_new = jnp.maximum(m_old, row_max if shift is None else row_max + shift)
    m_ref[idx] = m_new
    p = jnp.exp(s - jnp.concatenate([m_new if shift is None else m_new - shift] * n_chunks, axis=1))
    alpha = jnp.exp(m_old - m_new)
    lane_sum = p[:, :LANES]
    for c in range(1, n_chunks):
        lane_sum = lane_sum + p[:, c * LANES:(c + 1) * LANES]
    l_ref[idx] = alpha * l_ref[idx] + lane_sum
    return p, alpha


def _causal_mask(tb):
    return (lax.broadcasted_iota(jnp.int32, (tb, tb), 1) <= lax.broadcasted_iota(jnp.int32, (tb, tb), 0))


def _fox_prompt_kernel(tb, q_ref, k_ref, v_ref, c_ref, ct_ref, o_ref, kbd_ref, vbd_ref, m_ref, l_ref, acc_ref,
                       cq_ref):
    T = q_ref.shape[1]
    units = acc_ref.shape[0]
    lo16 = _head0_mask(BF16, tb)
    lo32 = _head0_mask(F32, tb)
    cols = [slice(u * LANES, (u + 1) * LANES) for u in range(units)]

    def prepare(j, carry):
        ks = pl.ds(pl.multiple_of(j * tb, tb), tb)
        for u in range(units):
            kbd_ref[u, j] = _block_diag_pair(k_ref[0, ks, cols[u]], lo16)
            vbd_ref[u, j] = _block_diag_pair(v_ref[0, ks, cols[u]], lo16)
        return carry

    lax.fori_loop(0, T // tb, prepare, 0)

    def tile(i, j, masked):
        qs = pl.ds(pl.multiple_of(i * tb, tb), tb)
        ks = pl.ds(pl.multiple_of(j * tb, tb), tb)
        for u in range(units):
            z = _dot_nt(q_ref[0, qs, cols[u]], kbd_ref[u, j])
            ps, alphas = [], []
            for h in range(2):
                s = z[:, h * tb:(h + 1) * tb] - ct_ref[0, u, h:h + 1, ks]
                if masked:
                    s = jnp.where(_causal_mask(tb), s, NEG)
                p, alpha = _softmax_step(s, m_ref, l_ref, 2 * u + h, cq_ref[2 * u + h])
                ps.append(p.astype(BF16))
                alphas.append(alpha)
            pv = _dot(jnp.concatenate(ps, axis=1), vbd_ref[u, j])
            acc_ref[u] = jnp.where(lo32, alphas[0], alphas[1]) * acc_ref[u] + pv

    def qblock(i, carry):
        qs = pl.ds(pl.multiple_of(i * tb, tb), tb)
        m_ref[...] = jnp.full_like(m_ref, NEG)
        l_ref[...] = jnp.zeros_like(l_ref)
        acc_ref[...] = jnp.zeros_like(acc_ref)
        for u in range(units):
            for h in range(2):
                cq_ref[2 * u + h] = jnp.broadcast_to(c_ref[0, u, qs, h:h + 1], (tb, LANES))
        tile(i, i, True)
        lax.fori_loop(0, i, lambda j, c: (tile(i, j, False), c)[1], 0)
        for u in range(units):
            denom = jnp.where(lo32, jnp.sum(l_ref[2 * u], axis=1, keepdims=True),
                              jnp.sum(l_ref[2 * u + 1], axis=1, keepdims=True))
            o_ref[0, qs, cols[u]] = (acc_ref[u] / denom).astype(o_ref.dtype)
        return carry

    lax.fori_loop(0, T // tb, qblock, 0)


def _fox_prompt(q, kb, vb, c, ct, first_pair, n_pairs, *, tb, units):
    B, T, _ = q.shape
    assert n_pairs % units == 0 and first_pair % units == 0
    w = units * LANES
    spec = pl.BlockSpec((1, T, w), lambda b, p: (b, 0, first_pair // units + p))
    c4 = c.reshape(B, T, n_pairs, 2).transpose(0, 2, 1, 3)
    ct4 = ct.reshape(B, n_pairs, 2, T)
    nb = T // tb
    return pl.pallas_call(
        functools.partial(_fox_prompt_kernel, tb), grid=(B, n_pairs // units),
        in_specs=[spec] * 3 + [pl.BlockSpec((1, units, T, 2), lambda b, p: (b, p, 0, 0)),
                               pl.BlockSpec((1, units, 2, T), lambda b, p: (b, p, 0, 0))],
        out_specs=pl.BlockSpec((1, T, w), lambda b, p: (b, 0, p)),
        out_shape=jax.ShapeDtypeStruct((B, T, n_pairs * LANES), BF16),
        scratch_shapes=[pltpu.VMEM((units, nb, 2 * tb, LANES), BF16), pltpu.VMEM((units, nb, 2 * tb, LANES), BF16),
                        pltpu.VMEM((2 * units, tb, LANES), F32), pltpu.VMEM((2 * units, tb, LANES), F32),
                        pltpu.VMEM((units, tb, LANES), F32), pltpu.VMEM((2 * units, tb, LANES), F32)],
        compiler_params=_cparams("parallel", "parallel"), name="fox_prompt",
    )(q, kb, vb, c4, ct4)


def _out_proj_kernel(n_parts, x_ref, *refs):
    o_ref = refs[-1]
    y = x_ref[...]
    for i in range(n_parts):
        y = y + _dot(refs[i][...], refs[n_parts + i][...])
    o_ref[...] = y


def _out_proj(x2, parts, w_out, *, tm):
    M, D = x2.shape
    ws, off = [], 0
    for p in parts:
        ws.append(w_out[off:off + p.shape[1]].astype(BF16))
        off += p.shape[1]
    n = len(parts)
    return pl.pallas_call(
        functools.partial(_out_proj_kernel, n), grid=(M // tm,),
        in_specs=[pl.BlockSpec((tm, D), lambda i: (i, 0))]
                 + [pl.BlockSpec((tm, p.shape[1]), lambda i: (i, 0)) for p in parts]
                 + [pl.BlockSpec(w.shape, lambda i: (0, 0)) for w in ws],
        out_specs=pl.BlockSpec((tm, D), lambda i: (i, 0)),
        out_shape=jax.ShapeDtypeStruct((M, D), F32),
        compiler_params=_cparams("parallel"), name="out_proj",
    )(x2, *parts, *ws)


FF_CHUNK = MXU_DIM
HALO = 2 * SUBLANES


def _silu(x):
    return x * (1.0 / (1.0 + jnp.exp(-x)))


def _ffn_chunk(cs, hcat, hb, g_prev, wg_ref, wu_ref, cw_ref, cb_ref, gs_ref, act_ref):
    tm = hb.shape[0]
    gx = _dot(hcat, wg_ref[:, cs])
    gs_ref[...] = gx
    g0 = gx[gx.shape[0] - tm:]
    g1 = g_prev(1, gs_ref[HALO - 1:HALO - 1 + tm, :])
    g2 = g_prev(2, gs_ref[HALO - 2:HALO - 2 + tm, :])
    gc = cb_ref[:, cs] + cw_ref[0:1, cs] * g2
    gc = gc + cw_ref[1:2, cs] * g1
    gc = gc + cw_ref[2:3, cs] * g0
    act_ref[:, cs] = (_silu(gc) * _dot(hb, wu_ref[:, cs])).astype(BF16)
    return g0


def _ffn_prompt_kernel(x_ref, halo_ref, g_ref, wg_ref, wu_ref, cw_ref, cb_ref, wd_ref, o_ref, conv_ref,
                       gs_ref, act_ref):
    t = pl.program_id(1)
    tm = x_ref.shape[1]
    x = x_ref[0]
    hb = _rmsnorm(x, g_ref[...]).astype(BF16)
    hh = jnp.where(t == 0, 0.0, _rmsnorm(halo_ref[0], g_ref[...])).astype(BF16)
    hcat = jnp.concatenate([hh, hb], axis=0)
    last = t == pl.num_programs(1) - 1
    for c in range(wg_ref.shape[1] // FF_CHUNK):
        cs = slice(c * FF_CHUNK, (c + 1) * FF_CHUNK)
        g0 = _ffn_chunk(cs, hcat, hb, lambda k, raw: raw, wg_ref, wu_ref, cw_ref, cb_ref, gs_ref, act_ref)

        @pl.when(last)
        def _():
            conv_ref[0, :, cs] = g0[tm - (CONV_W - 1):, :]
    o_ref[0] = x + _dot(act_ref[...], wd_ref[...])


def _ffn_sample_kernel(seq, x_ref, p1_ref, p2_ref, g_ref, wg_ref, wu_ref, cw_ref, cb_ref, wd_ref, o_ref, gate_ref,
                       gs_ref, act_ref):
    x = x_ref[...]
    tm = x.shape[0]
    hb = _rmsnorm(x, g_ref[...]).astype(BF16)
    hcat = jnp.concatenate([jnp.zeros((HALO, x.shape[1]), BF16), hb], axis=0)
    pos = lax.broadcasted_iota(jnp.int32, (tm, FF_CHUNK), 0) % seq
    for c in range(wg_ref.shape[1] // FF_CHUNK):
        cs = slice(c * FF_CHUNK, (c + 1) * FF_CHUNK)
        prev = lambda k, raw: jnp.where(pos < k, (p1_ref, p2_ref)[k - 1][:, cs], raw)
        gate_ref[:, cs] = _ffn_chunk(cs, hcat, hb, prev, wg_ref, wu_ref, cw_ref, cb_ref, gs_ref, act_ref)
    o_ref[...] = x + _dot(act_ref[...], wd_ref[...])


def _resident(a):
    zeros = (0,) * a.ndim
    return pl.BlockSpec(a.shape, lambda *_: zeros, pipeline_mode=pl.Buffered(1))


def _ffn_weights(g, w_gate, w_up, conv_w, conv_b, w_down):
    return (g.reshape(1, -1), w_gate.astype(BF16), w_up.astype(BF16), conv_w, conv_b.reshape(1, -1),
            w_down.astype(BF16))


def _ffn_prompt(x, g, w_gate, w_up, conv_w, conv_b, w_down, *, tm):
    B, T, D = x.shape
    dff = w_gate.shape[1]
    ws = _ffn_weights(g, w_gate, w_up, conv_w, conv_b, w_down)
    hpt = tm // HALO
    return pl.pallas_call(
        _ffn_prompt_kernel, grid=(B, T // tm),
        in_specs=[pl.BlockSpec((1, tm, D), lambda b, t: (b, t, 0)),
                  pl.BlockSpec((1, HALO, D), lambda b, t: (b, jnp.maximum(t * hpt - 1, 0), 0))]
                 + [_resident(w) for w in ws],
        out_specs=[pl.BlockSpec((1, tm, D), lambda b, t: (b, t, 0)),
                   pl.BlockSpec((1, CONV_W - 1, dff), lambda b, t: (b, 0, 0))],
        out_shape=[jax.ShapeDtypeStruct((B, T, D), F32), jax.ShapeDtypeStruct((B, CONV_W - 1, dff), F32)],
        scratch_shapes=[pltpu.VMEM((tm + HALO, FF_CHUNK), F32), pltpu.VMEM((tm, dff), BF16)],
        compiler_params=_cparams("parallel", "arbitrary"), name="ffn_prompt",
    )(x, x, *ws)


def _ffn_sample(x, state, g, w_gate, w_up, conv_w, conv_b, w_down):
    B, T, D = x.shape
    dff = w_gate.shape[1]
    ws = _ffn_weights(g, w_gate, w_up, conv_w, conv_b, w_down)
    zeros = jnp.zeros((B, T, dff), F32)
    p1 = zeros.at[:, 0].set(state[:, 1]).reshape(B * T, dff)
    p2 = zeros.at[:, 0].set(state[:, 0]).at[:, 1].set(state[:, 1]).reshape(B * T, dff)
    args = (x.reshape(B * T, D), p1, p2) + ws
    y, gate = pl.pallas_call(
        functools.partial(_ffn_sample_kernel, T), grid=(1,),
        in_specs=[_resident(a) for a in args],
        out_specs=[pl.BlockSpec((B * T, D), lambda i: (0, 0)), pl.BlockSpec((B * T, dff), lambda i: (0, 0))],
        out_shape=[jax.ShapeDtypeStruct((B * T, D), F32), jax.ShapeDtypeStruct((B * T, dff), F32)],
        scratch_shapes=[pltpu.VMEM((B * T + HALO, FF_CHUNK), F32), pltpu.VMEM((B * T, dff), BF16)],
        compiler_params=_cparams("arbitrary"), name="ffn_sample",
    )(*args)
    return y.reshape(B, T, D), gate.reshape(B, T, dff)[:, T - (CONV_W - 1):]


def _proj_diff_kernel(x_ref, g_ref, w_ref, qn_ref, kn_ref, gmat_ref, q_ref, k_ref, v_ref, kb_ref, vb_ref):
    wq = q_ref.shape[2]
    hb = _rmsnorm(x_ref[0], g_ref[...]).astype(BF16)
    gmat = gmat_ref[...]
    q_ref[0] = (_head_rmsnorm(_dot(hb, w_ref[:, :wq]), gmat, qn_ref[...]) * QK_SCALE).astype(BF16)
    k = _head_rmsnorm(_dot(hb, w_ref[:, wq:2 * wq]), gmat, kn_ref[...])
    k_ref[0] = k
    kb_ref[0] = k.astype(BF16)
    v = _dot(hb, w_ref[:, 2 * wq:])
    v_ref[0] = v
    vb_ref[0] = v.astype(BF16)


def _proj_diff(x, g, w_in, qn, kn, *, tm):
    B, T, D = x.shape
    wv = w_in.shape[1] // 3
    wq = wv
    reps = wq // HEAD_DIM
    args = (x, g.reshape(1, D), w_in.astype(BF16), jnp.tile(qn, reps).reshape(1, wq),
            jnp.tile(kn, reps).reshape(1, wq), _head_mean_matrix())
    full = lambda a: pl.BlockSpec(a.shape, lambda b, t: (0,) * a.ndim)
    row = lambda w: pl.BlockSpec((1, tm, w), lambda b, t: (b, t, 0))
    sds = lambda w, dt: jax.ShapeDtypeStruct((B, T, w), dt)
    return pl.pallas_call(
        _proj_diff_kernel, grid=(B, T // tm),
        in_specs=[row(D)] + [full(a) for a in args[1:]],
        out_specs=[row(wq), row(wq), row(wv), row(wq), row(wv)],
        out_shape=[sds(wq, BF16), sds(wq, F32), sds(wv, F32), sds(wq, BF16), sds(wv, BF16)],
        compiler_params=_cparams("parallel", "parallel"), name="proj_diff",
    )(*args)


def _t5_bucket(rel):
    rel = np.asarray(rel)
    max_exact = N_BUCKETS // 2
    relf = np.maximum(rel, 1).astype(np.float32)
    large = max_exact + (np.log(relf / np.float32(max_exact)) / np.float32(math.log(MAX_DISTANCE / max_exact))
                         * np.float32(N_BUCKETS - max_exact)).astype(np.int32)
    large = np.minimum(large, N_BUCKETS - 1)
    return np.where(rel < max_exact, rel, large).astype(np.int32)


def _near_buckets(q_pos):
    r = np.asarray(q_pos)[:, None]
    c = np.arange(BLK)[None, :]
    assert _t5_bucket(2 * BLK - (BLK - 1)) == N_BUCKETS - 1
    return jnp.asarray(np.stack([_t5_bucket(np.maximum(r - c, 0)), _t5_bucket(BLK + r - c)]))


def _diff_lambda(lq1_ref, lk1_ref, lq2_ref, lk2_ref, lam_init):
    s1 = jnp.sum(lq1_ref[...] * lk1_ref[...], axis=1, keepdims=True)
    s2 = jnp.sum(lq2_ref[...] * lk2_ref[...], axis=1, keepdims=True)
    return jnp.exp(s1) - jnp.exp(s2) + lam_init


def _diff_prompt_kernel(lam_init, rb_ref, q_ref, k_ref, v_ref, bk_ref, lq1_ref, lk1_ref, lq2_ref, lk2_ref,
                        sub_ref, o_ref, m_ref, l_ref, acc_ref, bias_ref, kbd_ref, vbd_ref):
    T = q_ref.shape[1]
    units, _, tb, _ = bias_ref.shape
    nsub = tb // BLK
    lo16 = _head0_mask(BF16, tb)
    lam = _diff_lambda(lq1_ref, lk1_ref, lq2_ref, lk2_ref, lam_init)
    cols = [slice(u * LANES, (u + 1) * LANES) for u in range(units)]
    heads = [pl.program_id(1) * units + u for u in range(units)]
    far_bias = [rb_ref[N_BUCKETS - 1, h] for h in heads]

    for u in range(units):
        near = []
        for d in range(2):
            tile_bias = jnp.zeros((BLK, BLK), F32)
            for b in range(N_BUCKETS):
                tile_bias = jnp.where(bk_ref[d] == b, rb_ref[b, heads[u]], tile_bias)
            near.append(tile_bias)
        bias_ref[u] = jnp.full(bias_ref.shape[1:], far_bias[u], F32)
        for a in range(nsub):
            bias_ref[u, 0, a * BLK:(a + 1) * BLK, a * BLK:(a + 1) * BLK] = near[0]
            if a >= 1:
                bias_ref[u, 0, a * BLK:(a + 1) * BLK, (a - 1) * BLK:a * BLK] = near[1]
        bias_ref[u, 1, 0:BLK, (nsub - 1) * BLK:nsub * BLK] = near[1]

    def prepare(j, carry):
        ks = pl.ds(pl.multiple_of(j * tb, tb), tb)
        for u in range(units):
            kbd_ref[u, j] = _block_diag_pair(k_ref[0, ks, cols[u]], lo16)
            vb = v_ref[0, ks, cols[u]]
            zero = jnp.zeros_like(vb)
            vbd_ref[u, j] = jnp.concatenate([jnp.concatenate([vb, zero], axis=1),
                                             jnp.concatenate([zero, vb], axis=1)], axis=0)
        return carry

    lax.fori_loop(0, T // tb, prepare, 0)

    def tile(i, j, near, masked):
        qs = pl.ds(pl.multiple_of(i * tb, tb), tb)
        for u in range(units):
            z = _dot_nt(q_ref[0, qs, cols[u]], kbd_ref[u, j])
            ps, alphas = [], []
            for mp in range(2):
                s = z[:, mp * tb:(mp + 1) * tb]
                if near is not None:
                    s = s + bias_ref[u, near]
                if masked:
                    s = jnp.where(_causal_mask(tb), s, NEG)
                p, alpha = _softmax_step(s, m_ref, l_ref, 2 * u + mp, None if near is not None else far_bias[u])
                ps.append(p.astype(BF16))
                alphas.append(alpha)
            pv = _dot(jnp.concatenate(ps, axis=1), vbd_ref[u, j])
            acc_ref[u] = jnp.concatenate(alphas, axis=1) * acc_ref[u] + pv

    def qblock(i, carry):
        qs = pl.ds(pl.multiple_of(i * tb, tb), tb)
        m_ref[...] = jnp.full_like(m_ref, NEG)
        l_ref[...] = jnp.zeros_like(l_ref)
        acc_ref[...] = jnp.zeros_like(acc_ref)
        tile(i, i, 0, True)

        @pl.when(i >= 1)
        def _():
            tile(i, i - 1, 1, False)

        lax.fori_loop(0, jnp.maximum(i - 1, 0), lambda j, c: (tile(i, j, None, False), c)[1], 0)
        for u in range(units):
            o_map = [acc_ref[u, :, mp * LANES:(mp + 1) * LANES] / jnp.sum(l_ref[2 * u + mp], axis=1, keepdims=True)
                     for mp in range(2)]
            o = o_map[0] - lam * o_map[1]
            o_ref[0, qs, cols[u]] = (_rmsnorm(o, sub_ref[...]) * (1.0 - lam_init)).astype(o_ref.dtype)
        return carry

    lax.fori_loop(0, T // tb, qblock, 0)


def _diff_prompt(q, kb, vb, rel_bias, lq1, lk1, lq2, lk2, subln, lam_init, *, tb, units):
    B, T, W = q.shape
    n_heads = W // LANES
    assert n_heads % units == 0
    buckets = _near_buckets(np.arange(BLK))
    spec = pl.BlockSpec((1, T, units * LANES), lambda b, h: (b, 0, h))
    small = lambda a: pl.BlockSpec(a.shape, lambda b, h: (0,) * a.ndim)
    vecs = [a.reshape(1, -1) for a in (lq1, lk1, lq2, lk2, subln)]
    nb = T // tb
    return pl.pallas_call(
        functools.partial(_diff_prompt_kernel, lam_init), grid=(B, n_heads // units),
        in_specs=[pl.BlockSpec(memory_space=pltpu.SMEM)] + [spec] * 3 + [small(buckets)] + [small(a) for a in vecs],
        out_specs=spec,
        scratch_shapes=[pltpu.VMEM((2 * units, tb, LANES), F32), pltpu.VMEM((2 * units, tb, LANES), F32),
                        pltpu.VMEM((units, tb, 2 * LANES), F32), pltpu.VMEM((units, 2, tb, tb), F32),
                        pltpu.VMEM((units, nb, 2 * tb, LANES), BF16),
                        pltpu.VMEM((units, nb, 2 * tb, 2 * LANES), BF16)],
        out_shape=jax.ShapeDtypeStruct((B, T, W), BF16),
        compiler_params=_cparams("parallel", "parallel"), name="diff_prompt",
    )(rel_bias, q, kb, vb, buckets, *vecs)


GROUPS_PER_CHUNK = MXU_DIM // HEAD_DIM


def _chunked_queries(q):
    B, T, W = q.shape
    qg = q.reshape(B, T, W // MXU_DIM, GROUPS_PER_CHUNK, HEAD_DIM).transpose(0, 2, 3, 1, 4)
    eye = jnp.eye(GROUPS_PER_CHUNK, dtype=q.dtype)
    out = qg[:, :, :, :, None, :] * eye[None, None, :, None, :, None]
    return out.reshape(B, W // MXU_DIM, GROUPS_PER_CHUNK * T, MXU_DIM)


def _keys_on_lanes(cache):
    nd = cache.ndim
    t = cache.transpose((0, 1) + tuple(range(3, nd)) + (2,))
    return t.reshape(cache.shape[0], cache.shape[1], -1, cache.shape[2])


def _scores(qc_ref, key_tiles, transposed):
    rows = []
    for c in range(qc_ref.shape[1]):
        cs = slice(c * MXU_DIM, (c + 1) * MXU_DIM)
        if transposed:
            rows.append(_dot(qc_ref[0, c], jnp.concatenate([kt[cs, :].astype(BF16) for kt in key_tiles], axis=1)))
        else:
            rows.append(_dot_nt(qc_ref[0, c], jnp.concatenate([kt[:, cs].astype(BF16) for kt in key_tiles], axis=0)))
    return jnp.concatenate(rows, axis=0)


def _mix_sample_kernel(seq, n_sb, n_pages, pt_ref, qc_ref, kn_ref, vn_ref, lfn_ref, *refs):
    kp = refs[:n_pages]
    vp = refs[n_pages:2 * n_pages]
    lfp = refs[2 * n_pages:3 * n_pages]
    o_ref, run_ref, m_ref, l_ref, acc_ref = refs[3 * n_pages:]
    j = pl.program_id(1)
    sb = n_sb * seq
    n_fox = (BLK - sb) // seq
    rpc = GROUPS_PER_CHUNK * seq
    n_chunks = qc_ref.shape[1]
    suffix = _suffix_matrix()

    def block(z, lfts, pv_of, mask):
        n = len(lfts)
        zs, zf = z[:sb], z[sb:]
        ls = _log_sigmoid_scores(zs)
        l1m = ls - zs
        a_parts, s_parts = [], []
        for g in range(n):
            gs = slice(g * BLK, (g + 1) * BLK)
            elf = jnp.concatenate([jnp.broadcast_to(lfts[g][h:h + 1, :], (seq, BLK)) for h in range(n_fox)], axis=0)
            x = jnp.concatenate([l1m[:, gs], elf], axis=0)
            if mask is not None:
                x = jnp.where(mask, x, 0.0)
            cs = _dot(_split(x), suffix)
            run = run_ref[...]
            run_ref[...] = run + cs[:, BLK:]
            later = cs[:, :BLK] + run
            a = jnp.exp(ls[:, gs] + later[:sb])
            s = zf[:, gs] + later[sb:]
            if mask is not None:
                a = jnp.where(mask[:sb], a, 0.0)
                s = jnp.where(mask[sb:], s, NEG)
            a_parts.append(a)
            s_parts.append(s)
        p, alpha = _softmax_step(jnp.concatenate(s_parts, axis=1), m_ref, l_ref, 0)
        w = jnp.concatenate([jnp.concatenate(a_parts, axis=1), p], axis=0).astype(BF16)
        for c in range(n_chunks):
            pv = pv_of(c, w[c * rpc:(c + 1) * rpc])
            if (c + 1) * rpc <= sb:
                acc_ref[c] += pv
            else:
                al = alpha[c * rpc - sb:(c + 1) * rpc - sb]
                acc_ref[c] = jnp.concatenate([al] * (MXU_DIM // LANES), axis=1) * acc_ref[c] + pv

    @pl.when(j == 0)
    def _():
        run_ref[...] = jnp.zeros_like(run_ref)
        m_ref[...] = jnp.full_like(m_ref, NEG)
        l_ref[...] = jnp.zeros_like(l_ref)
        acc_ref[...] = jnp.zeros_like(acc_ref)
        lane = lax.broadcasted_iota(jnp.int32, (BLK, BLK), 1)
        row = lax.broadcasted_iota(jnp.int32, (BLK, BLK), 0)
        own = lane < row % seq + jnp.where(row < sb, 0, 1)
        kn = _pad_rows(kn_ref[0], BLK)
        vn = _pad_rows(vn_ref[0], BLK).astype(BF16)
        block(_scores(qc_ref, [kn], False), [lfn_ref[0]],
              lambda c, w: _dot(w, vn[:, c * MXU_DIM:(c + 1) * MXU_DIM]), own)

    def page_values(c, w):
        cs = slice(c * MXU_DIM, (c + 1) * MXU_DIM)
        return _dot_nt(w, jnp.concatenate([v[cs, :].astype(BF16) for v in vp], axis=1))

    block(_scores(qc_ref, kp, True), [r[...] for r in lfp], page_values, None)

    @pl.when(j == pl.num_programs(1) - 1)
    def _():
        col_group = lax.broadcasted_iota(jnp.int32, (seq, MXU_DIM), 1) // HEAD_DIM
        denom = jnp.sum(l_ref[0], axis=1, keepdims=True)
        outs = []
        for c in range(n_chunks):
            rows = acc_ref[c]
            if (c + 1) * rpc > sb:
                rows = rows / denom[c * rpc - sb:(c + 1) * rpc - sb]
            out = jnp.zeros((seq, MXU_DIM), F32)
            for g in range(GROUPS_PER_CHUNK):
                out = jnp.where(col_group == g, rows[g * seq:(g + 1) * seq, :], out)
            outs.append(out)
        o_ref[0] = jnp.concatenate(outs, axis=1)


def _paged_specs(shape_tail, layer, n_total, per_step):
    def spec(g):
        zeros = (0,) * len(shape_tail)
        return pl.BlockSpec((None, None) + shape_tail,
                            lambda b, j, pt: (layer, pt[b, n_total - 1 - (j * per_step + g)]) + zeros)
    return [spec(g) for g in range(per_step)]


def _mix_sample(q, k_new, v_new, logf_new, cache_k, cache_v, cache_logf, layer, page_table, n_sb):
    B, T, W = q.shape
    n_heads = W // HEAD_DIM
    n_fox = n_heads - n_sb
    n_total = page_table.shape[1]
    g = min(PAGES_PER_STEP, n_total)
    assert n_heads * T == BLK and cache_k.shape[2] == BLK and n_total % g == 0
    assert (n_sb * T) % (GROUPS_PER_CHUNK * T) == 0
    lfn = jnp.zeros((B, n_fox, BLK), F32).at[:, :, :T].set(logf_new.transpose(0, 2, 1))
    args = (_chunked_queries(q), k_new, v_new, lfn)
    per_seq = lambda a: pl.BlockSpec((1,) + a.shape[1:], lambda b, j, pt: (b,) + (0,) * (a.ndim - 1))
    return pl.pallas_call(
        functools.partial(_mix_sample_kernel, T, n_sb, g),
        grid_spec=pltpu.PrefetchScalarGridSpec(
            num_scalar_prefetch=1, grid=(B, n_total // g),
            in_specs=[per_seq(a) for a in args] + _paged_specs((W, BLK), layer, n_total, g) * 2
                     + _paged_specs((n_fox, BLK), layer, n_total, g),
            out_specs=pl.BlockSpec((1, T, W), lambda b, j, pt: (b, 0, 0)),
            scratch_shapes=[pltpu.VMEM((BLK, BLK), F32), pltpu.VMEM((1, n_fox * T, BLK), F32),
                            pltpu.VMEM((1, n_fox * T, BLK), F32),
                            pltpu.VMEM((W // MXU_DIM, GROUPS_PER_CHUNK * T, MXU_DIM), F32)]),
        out_shape=jax.ShapeDtypeStruct((B, T, W), F32),
        compiler_params=_cparams("parallel", "arbitrary"), name="mix_sample",
    )(page_table, *args, *([_keys_on_lanes(cache_k)] * g), *([_keys_on_lanes(cache_v)] * g),
      *([_keys_on_lanes(cache_logf)] * g))


def _diff_sample_kernel(seq, lam_init, n_pages, pt_ref, qc_ref, kn_ref, vn_ref, bk_ref, rbr_ref,
                        lq1_ref, lk1_ref, lq2_ref, lk2_ref, sub_ref, *refs):
    kp = refs[:n_pages]
    vp = refs[n_pages:2 * n_pages]
    o_ref, m_ref, l_ref, acc_ref, bias_ref = refs[2 * n_pages:]
    j = pl.program_id(1)
    n_heads = vn_ref.shape[2] // BLK
    rph = 2 * seq

    def near_bias(d):
        out = jnp.zeros((BLK, BLK), F32)
        for b in range(N_BUCKETS):
            out = jnp.where(bk_ref[d] == b, rbr_ref[:, b:b + 1], out)
        return out

    def block(s, values_of, mask):
        if mask is not None:
            s = jnp.where(mask, s, NEG)
        p, alpha = _softmax_step(s, m_ref, l_ref, 0)
        w = p.astype(BF16)
        pv = jnp.concatenate([_dot(w[h * rph:(h + 1) * rph], values_of(h)) for h in range(n_heads)], axis=0)
        acc_ref[...] = alpha * acc_ref[...] + pv

    far = jnp.broadcast_to(rbr_ref[:, N_BUCKETS - 1:N_BUCKETS], (BLK, BLK))

    @pl.when(j == 0)
    def _():
        m_ref[...] = jnp.full_like(m_ref, NEG)
        l_ref[...] = jnp.zeros_like(l_ref)
        acc_ref[...] = jnp.zeros_like(acc_ref)
        lane = lax.broadcasted_iota(jnp.int32, (BLK, BLK), 1)
        row = lax.broadcasted_iota(jnp.int32, (BLK, BLK), 0)
        kn = _pad_rows(kn_ref[0], BLK)
        block(_scores(qc_ref, [kn], False) + near_bias(0),
              lambda h: _pad_rows(vn_ref[0, :, h * BLK:(h + 1) * BLK], BLK).astype(BF16), lane <= row % seq)
        bias_ref[...] = jnp.concatenate([near_bias(1)] + [far] * (n_pages - 1), axis=1)

    @pl.when(j == 1)
    def _():
        bias_ref[...] = jnp.concatenate([far] * n_pages, axis=1)

    block(_scores(qc_ref, kp, True) + bias_ref[...],
          lambda h: jnp.concatenate([v[pl.ds(h, BLK, stride=n_heads), :].astype(BF16) for v in vp], axis=0), None)

    @pl.when(j == pl.num_programs(1) - 1)
    def _():
        rows = acc_ref[...] / jnp.sum(l_ref[0], axis=1, keepdims=True)
        lam = _diff_lambda(lq1_ref, lk1_ref, lq2_ref, lk2_ref, lam_init)
        sub = sub_ref[...]
        outs = []
        for h in range(n_heads):
            o = rows[h * rph:h * rph + seq] - lam * rows[h * rph + seq:(h + 1) * rph]
            outs.append(_rmsnorm(o, sub))
        o_ref[0] = jnp.concatenate(outs, axis=1) * (1.0 - lam_init)


def _diff_sample(q, k_new, v_new, cache_k, cache_v, layer, page_table, rel_bias, lq1, lk1, lq2, lk2, subln, lam_init):
    B, T, W = q.shape
    n_heads = cache_v.shape[3]
    n_total = page_table.shape[1]
    g = min(PAGES_PER_STEP, n_total)
    assert (W // HEAD_DIM) * T == BLK and cache_k.shape[2] == BLK and n_total % g == 0
    assert cache_v.shape[4] == BLK and v_new.shape[2] == n_heads * BLK
    L, P = cache_v.shape[:2]
    v_rows = cache_v.reshape(L, P, BLK * n_heads, BLK)
    buckets = _near_buckets(np.arange(BLK) % T)
    rb_rows = jnp.repeat(rel_bias.T, 2 * T, axis=0)
    vecs = [a.reshape(1, -1) for a in (lq1, lk1, lq2, lk2, subln)]
    args = (_chunked_queries(q), k_new, v_new)
    per_seq = lambda a: pl.BlockSpec((1,) + a.shape[1:], lambda b, j, pt: (b,) + (0,) * (a.ndim - 1))
    small = lambda a: pl.BlockSpec(a.shape, lambda b, j, pt: (0,) * a.ndim)
    return pl.pallas_call(
        functools.partial(_diff_sample_kernel, T, lam_init, g),
        grid_spec=pltpu.PrefetchScalarGridSpec(
            num_scalar_prefetch=1, grid=(B, n_total // g),
            in_specs=[per_seq(a) for a in args] + [small(buckets), small(rb_rows)] + [small(a) for a in vecs]
                     + _paged_specs((W, BLK), layer, n_total, g) + _paged_specs((BLK * n_heads, BLK), layer, n_total, g),
            out_specs=pl.BlockSpec((1, T, n_heads * BLK), lambda b, j, pt: (b, 0, 0)),
            scratch_shapes=[pltpu.VMEM((1, BLK, BLK), F32), pltpu.VMEM((1, BLK, BLK), F32),
                            pltpu.VMEM((BLK, BLK), F32), pltpu.VMEM((BLK, g * BLK), F32)]),
        out_shape=jax.ShapeDtypeStruct((B, T, n_heads * BLK), F32),
        compiler_params=_cparams("parallel", "arbitrary"), name="diff_sample",
    )(page_table, *args, buckets, rb_rows, *vecs, *([_keys_on_lanes(cache_k)] * g), *([v_rows] * g))


def kernel(x_prompt, x_sample, cache_k_mix, cache_v_mix, cache_logf_mix, cache_k_diff, cache_v_diff, state_ffn_conv, page_table, rel_bias, norm_mix, w_in_mix, b_forget, qnorm_fox, knorm_fox, w_out_mix, norm_diff, w_in_diff, qnorm_diff, knorm_diff, lambda_q1, lambda_k1, lambda_q2, lambda_k2, subln_diff, w_out_diff, norm_ffn, w_gate, w_up, conv_w, conv_b, w_down):
    B, T, D = x_prompt.shape
    DB, DT, _ = x_sample.shape
    depth = norm_ffn.shape[0]
    n_fox = b_forget.shape[1]
    n_mix = cache_k_mix.shape[3]
    n_sb = n_mix - n_fox
    n_diff = cache_k_diff.shape[3]
    sb_pairs = n_sb * HEAD_DIM // LANES
    fox_pairs = n_fox * HEAD_DIM // LANES
    tm = min(T, PROMPT_ROW_TILE)
    tb = min(T, ATT_BLOCK)
    xp, xs = x_prompt, x_sample
    outs = {n: [] for n in ("kmp", "vmp", "lfp", "kms", "vms", "lfs", "kdp", "vdp", "kds", "vds", "cp", "cs")}
    for l in range(depth):
        j = l // 2
        if l % 2 == 0:
            pw = (norm_mix[j], w_in_mix[j], b_forget[j], qnorm_fox[j], knorm_fox[j])
            q, k, v, kb, vb, logf, c, ct = _proj_mix(xp, *pw, tm=tm, with_cumsum=True)
            o_sb = _sb_prompt(q, kb, vb, sb_pairs, group=min(T // BLK, SB_GROUP))
            o_fx = _fox_prompt(q, kb, vb, c, ct, sb_pairs, fox_pairs, tb=tb, units=ATT_UNITS)
            xp = _out_proj(xp.reshape(B * T, D), [o_sb.reshape(B * T, -1), o_fx.reshape(B * T, -1)], w_out_mix[j],
                           tm=tm).reshape(B, T, D)
            outs["kmp"].append(k.reshape(B, T, n_mix, HEAD_DIM))
            outs["vmp"].append(v.reshape(B, T, n_mix, HEAD_DIM))
            outs["lfp"].append(logf)
            qs, ks, vs, _, _, lfs = _proj_mix(xs.reshape(1, DB * DT, D), *pw, tm=DB * DT, with_cumsum=False)
            ks, vs, lfs = ks.reshape(DB, DT, -1), vs.reshape(DB, DT, -1), lfs.reshape(DB, DT, n_fox)
            o = _mix_sample(qs.reshape(DB, DT, -1), ks, vs, lfs, cache_k_mix, cache_v_mix, cache_logf_mix, j,
                            page_table, n_sb)
            xs = _out_proj(xs.reshape(DB * DT, D), [o.reshape(DB * DT, -1).astype(BF16)], w_out_mix[j],
                           tm=DB * DT).reshape(DB, DT, D)
            outs["kms"].append(ks.reshape(DB, DT, n_mix, HEAD_DIM))
            outs["vms"].append(vs.reshape(DB, DT, n_mix, HEAD_DIM))
            outs["lfs"].append(lfs)
        else:
            lam_init = 0.8 - 0.6 * math.exp(-0.3 * l)
            pw = (norm_diff[j], w_in_diff[j], qnorm_diff[j], knorm_diff[j])
            lam_w = (lambda_q1[j], lambda_k1[j], lambda_q2[j], lambda_k2[j], subln_diff[j], lam_init)
            q, k, v, kb, vb = _proj_diff(xp, *pw, tm=tm)
            o = _diff_prompt(q, kb, vb, rel_bias, *lam_w, tb=tb, units=ATT_UNITS)
            xp = _out_proj(xp.reshape(B * T, D), [o.reshape(B * T, -1)], w_out_diff[j], tm=tm).reshape(B, T, D)
            outs["kdp"].append(k.reshape(B, T, n_diff, 2, HEAD_DIM))
            outs["vdp"].append(v.reshape(B, T, n_diff, 2 * HEAD_DIM))
            qs, ks, vs, _, _ = _proj_diff(xs.reshape(1, DB * DT, D), *pw, tm=DB * DT)
            ks, vs = ks.reshape(DB, DT, -1), vs.reshape(DB, DT, -1)
            o = _diff_sample(qs.reshape(DB, DT, -1), ks, vs, cache_k_diff, cache_v_diff, j, page_table, rel_bias,
                             *lam_w)
            xs = _out_proj(xs.reshape(DB * DT, D), [o.reshape(DB * DT, -1).astype(BF16)], w_out_diff[j],
                           tm=DB * DT).reshape(DB, DT, D)
            outs["kds"].append(ks.reshape(DB, DT, n_diff, 2, HEAD_DIM))
            outs["vds"].append(vs.reshape(DB, DT, n_diff, 2 * HEAD_DIM))
        fw = (norm_ffn[l], w_gate[l], w_up[l], conv_w[l], conv_b[l], w_down[l])
        xp, cp = _ffn_prompt(xp, *fw, tm=min(T, FFN_ROW_TILE))
        xs, cs = _ffn_sample(xs, state_ffn_conv[l], *fw)
        outs["cp"].append(cp)
        outs["cs"].append(cs)
    st = {n: jnp.stack(a) for n, a in outs.items()}
    return (xp, xs, st["kmp"], st["vmp"], st["lfp"], st["kms"], st["vms"], st["lfs"], st["kdp"], st["vdp"],
            st["kds"], st["vds"], st["cp"], st["cs"])
```

```python
import functools
import math

import numpy as np
import jax
import jax.numpy as jnp
from jax import lax
from jax.experimental import pallas as pl
from jax.experimental.pallas import tpu as pltpu

F32 = jnp.float32
BF16 = jnp.bfloat16

HEAD_DIM = 64
N_BUCKETS = 32
MAX_DISTANCE = 128
CONV_W = 3
EPS = 1e-6
QK_SCALE = HEAD_DIM ** -0.5

LANES = 128
SUBLANES = 8
MXU_DIM = 256
VMEM_LIMIT = 56 * 1024 * 1024

BLK = LANES
NEG = -1e30
EXP_ZERO_BELOW = -104.0

PROMPT_ROW_TILE = 512
FFN_ROW_TILE = 1024
ATT_BLOCK = 512
ATT_UNITS = 2
SB_GROUP = 8
PAGES_PER_STEP = 16


def _cparams(*sem):
    return pltpu.CompilerParams(dimension_semantics=sem, vmem_limit_bytes=VMEM_LIMIT)


def _dot(a, b):
    return jnp.dot(a, b, preferred_element_type=F32)


def _dot_nt(a, b):
    return lax.dot_general(a, b, (((1,), (1,)), ((), ())), preferred_element_type=F32)


def _dot_f32(a, b):
    return jnp.dot(a, b, preferred_element_type=F32, precision=lax.Precision.HIGHEST)


def _split(a):
    hi = a.astype(BF16)
    lo = (a - hi.astype(F32)).astype(BF16)
    return jnp.concatenate([hi, lo], axis=1)


def _log_sigmoid(x):
    return jnp.minimum(x, 0.0) - jnp.log1p(jnp.exp(-jnp.abs(x)))


def _log_sigmoid_scores(x):
    return jnp.minimum(x, 0.0) - jnp.log(1.0 + jnp.exp(-jnp.abs(x)))


def _rmsnorm(x, g):
    return x * lax.rsqrt(jnp.mean(x * x, axis=-1, keepdims=True) + EPS) * g


def _head_rmsnorm(y, gmat2, gain):
    outs = []
    for c in range(y.shape[1] // LANES):
        yc = y[:, c * LANES:(c + 1) * LANES]
        outs.append(yc * lax.rsqrt(_dot(_split(yc * yc), gmat2) + EPS))
    return jnp.concatenate(outs, axis=1) * gain


def _head_mean_matrix():
    idx = np.arange(LANES) // HEAD_DIM
    g = (idx[:, None] == idx[None, :]).astype(np.float32) / HEAD_DIM
    return jnp.asarray(np.concatenate([g, g], axis=0), BF16)


def _suffix_matrix():
    r = lax.broadcasted_iota(jnp.int32, (2 * BLK, 2 * BLK), 0) & (BLK - 1)
    c = lax.broadcasted_iota(jnp.int32, (2 * BLK, 2 * BLK), 1)
    return jnp.where((c >= BLK) | (r > c), 1.0, 0.0).astype(BF16)


def _head0_mask(dtype, rows=BLK):
    lane = lax.broadcasted_iota(jnp.int32, (rows, LANES), 1)
    if dtype == F32:
        return lane < HEAD_DIM
    return jnp.where(lane < HEAD_DIM, 1.0, 0.0).astype(dtype) > 0


def _block_diag_pair(x, lo_half):
    zero = jnp.zeros_like(x)
    return jnp.concatenate([jnp.where(lo_half, x, zero), jnp.where(lo_half, zero, x)], axis=0)


def _pad_rows(x, n):
    return jnp.concatenate([x, jnp.zeros((n - x.shape[0],) + x.shape[1:], x.dtype)], axis=0)


def _proj_mix_kernel(with_cumsum, x_ref, g_ref, w_ref, wf_ref, wft_ref, bfr_ref, bfc_ref, qn_ref, kn_ref,
                     gmat_ref, q_ref, k_ref, v_ref, kb_ref, vb_ref, logf_ref, *rest):
    ws = q_ref.shape[2] // 2
    n_fox = logf_ref.shape[2]
    tm = x_ref.shape[1]
    hb = _rmsnorm(x_ref[0], g_ref[...]).astype(BF16)

    def sec(i):
        return _dot(hb, w_ref[:, i * ws:(i + 1) * ws])

    q_ref[0, :, :ws] = (sec(0) * QK_SCALE).astype(BF16)
    k_sb = sec(1)
    k_ref[0, :, :ws] = k_sb
    kb_ref[0, :, :ws] = k_sb.astype(BF16)
    v_sb = sec(2)
    v_ref[0, :, :ws] = v_sb
    vb_ref[0, :, :ws] = v_sb.astype(BF16)
    gmat = gmat_ref[...]
    q_ref[0, :, ws:] = (_head_rmsnorm(sec(3), gmat, qn_ref[...]) * QK_SCALE).astype(BF16)
    k_fx = _head_rmsnorm(sec(4), gmat, kn_ref[...])
    k_ref[0, :, ws:] = k_fx
    kb_ref[0, :, ws:] = k_fx.astype(BF16)
    v_fx = sec(5)
    v_ref[0, :, ws:] = v_fx
    vb_ref[0, :, ws:] = v_fx.astype(BF16)

    logf = _log_sigmoid(_dot(hb, wf_ref[...]) + bfr_ref[...])
    logf_ref[0] = logf[:, :n_fox]
    if not with_cumsum:
        return
    c_ref, ct_ref, carry_ref, carryt_ref = rest

    @pl.when(pl.program_id(1) == 0)
    def _():
        carry_ref[...] = jnp.zeros_like(carry_ref)
        carryt_ref[...] = jnp.zeros_like(carryt_ref)

    r = lax.broadcasted_iota(jnp.int32, (tm, tm), 0)
    c = lax.broadcasted_iota(jnp.int32, (tm, tm), 1)
    csum = _dot_f32(jnp.where(c <= r, 1.0, 0.0).astype(F32), logf) + carry_ref[...]
    c_ref[0] = csum[:, :n_fox]
    carry_ref[...] = csum[tm - 1:tm, :]
    logft = _log_sigmoid(_dot_nt(wft_ref[...], hb) + bfc_ref[:, 0:1])
    csumt = _dot_f32(logft, jnp.where(r <= c, 1.0, 0.0).astype(F32)) + carryt_ref[:, 0:1]
    ct_ref[0] = csumt[:n_fox, :]
    carryt_ref[...] = jnp.broadcast_to(csumt[:, tm - 1:tm], carryt_ref.shape)


def _proj_mix(x, g, w_in, b_f, qn, kn, *, tm, with_cumsum):
    B, T, D = x.shape
    n_fox = b_f.shape[0]
    ws = (w_in.shape[1] - n_fox) // 6
    wmain = w_in[:, :6 * ws].astype(BF16)
    wf = jnp.zeros((D, LANES), BF16).at[:, :n_fox].set(w_in[:, 6 * ws:].astype(BF16))
    wft = jnp.zeros((2 * SUBLANES, D), BF16).at[:n_fox, :].set(w_in[:, 6 * ws:].T.astype(BF16))
    bfr = jnp.zeros((1, LANES), F32).at[0, :n_fox].set(b_f)
    bfc = jnp.zeros((2 * SUBLANES, LANES), F32).at[:n_fox, :].set(jnp.broadcast_to(b_f[:, None], (n_fox, LANES)))
    reps = ws // HEAD_DIM
    full = lambda a: pl.BlockSpec(a.shape, lambda b, t: (0,) * a.ndim)
    row = lambda w: pl.BlockSpec((1, tm, w), lambda b, t: (b, t, 0))
    args = (x, g.reshape(1, D), wmain, wf, wft, bfr, bfc, jnp.tile(qn, reps).reshape(1, ws),
            jnp.tile(kn, reps).reshape(1, ws), _head_mean_matrix())
    out_shape = [jax.ShapeDtypeStruct((B, T, 2 * ws), BF16), jax.ShapeDtypeStruct((B, T, 2 * ws), F32),
                 jax.ShapeDtypeStruct((B, T, 2 * ws), F32), jax.ShapeDtypeStruct((B, T, 2 * ws), BF16),
                 jax.ShapeDtypeStruct((B, T, 2 * ws), BF16), jax.ShapeDtypeStruct((B, T, n_fox), F32)]
    out_specs = [row(2 * ws)] * 5 + [row(n_fox)]
    scratch = []
    if with_cumsum:
        out_shape += [jax.ShapeDtypeStruct((B, T, n_fox), F32), jax.ShapeDtypeStruct((B, n_fox, T), F32)]
        out_specs += [row(n_fox), pl.BlockSpec((1, n_fox, tm), lambda b, t: (b, 0, t))]
        scratch = [pltpu.VMEM((1, LANES), F32), pltpu.VMEM((2 * SUBLANES, LANES), F32)]
    return pl.pallas_call(
        functools.partial(_proj_mix_kernel, with_cumsum),
        grid=(B, T // tm),
        in_specs=[row(D)] + [full(a) for a in args[1:]],
        out_specs=out_specs, out_shape=out_shape, scratch_shapes=scratch,
        compiler_params=_cparams("parallel", "arbitrary"), name="proj_mix",
    )(*args)


def _sb_prompt_kernel(q_ref, k_ref, v_ref, o_ref, r_ref, acc_ref):
    T = q_ref.shape[1]
    group = r_ref.shape[0]
    lane = lax.broadcasted_iota(jnp.int32, (BLK, BLK), 1)
    row = lax.broadcasted_iota(jnp.int32, (BLK, BLK), 0)
    lo_half = _head0_mask(BF16)
    strict = lane < row
    strict2 = jnp.concatenate([strict, strict], axis=1)
    suffix = _suffix_matrix()

    def tile(a, iq, ik, valid, masked):
        q2 = q_ref[0, pl.ds(pl.multiple_of(iq * BLK, BLK), BLK), :]
        ks = pl.ds(pl.multiple_of(ik * BLK, BLK), BLK)
        z = _dot_nt(q2, _block_diag_pair(k_ref[0, ks, :], lo_half))
        ls = _log_sigmoid_scores(z)
        l1m = ls - z
        if masked:
            l1m = jnp.where(strict2, l1m, 0.0)
        w = []
        for h in range(2):
            hs = slice(h * BLK, (h + 1) * BLK)
            cs = _dot(_split(l1m[:, hs]), suffix)
            run = r_ref[a, :, hs]
            wh = jnp.exp(ls[:, hs] + cs[:, :BLK] + run)
            total = cs[:, BLK:]
            if masked:
                wh = jnp.where(strict, wh, 0.0)
            if valid is not None:
                wh = wh * valid
                total = total * valid
            w.append(wh.astype(BF16))
            r_ref[a, :, hs] = run + total
        acc_ref[a] += _dot(jnp.concatenate(w, axis=1), _block_diag_pair(v_ref[0, ks, :], lo_half))

    def qgroup(i, carry):
        base = i * group
        r_ref[...] = jnp.zeros_like(r_ref)
        acc_ref[...] = jnp.zeros_like(acc_ref)
        for a in range(group):
            tile(a, base + a, base + a, None, True)

        def alive():
            return (jnp.max(r_ref[...]) > EXP_ZERO_BELOW).astype(jnp.int32)

        def cond(s):
            return (s[0] < base + group) & (s[1] > 0)

        def body(s):
            d = s[0]
            for a in range(group):
                ik = base + a - d
                tile(a, base + a, jnp.maximum(ik, 0), (ik >= 0).astype(F32), False)
            return d + 1, alive()

        lax.while_loop(cond, body, (1, alive()))
        for a in range(group):
            o_ref[0, pl.ds(pl.multiple_of((base + a) * BLK, BLK), BLK), :] = acc_ref[a].astype(o_ref.dtype)
        return carry

    lax.fori_loop(0, T // (group * BLK), qgroup, 0)


def _sb_prompt(q, kb, vb, n_pairs, *, group):
    B, T, _ = q.shape
    spec = pl.BlockSpec((1, T, LANES), lambda b, p: (b, 0, p))
    return pl.pallas_call(
        _sb_prompt_kernel, grid=(B, n_pairs), in_specs=[spec] * 3, out_specs=spec,
        out_shape=jax.ShapeDtypeStruct((B, T, n_pairs * LANES), BF16),
        scratch_shapes=[pltpu.VMEM((group, BLK, 2 * BLK), F32), pltpu.VMEM((group, BLK, BLK), F32)],
        compiler_params=_cparams("parallel", "parallel"), name="sb_prompt",
    )(q, kb, vb)


def _softmax_step(s, m_ref, l_ref, idx, shift=None):
    n_chunks = s.shape[1] // LANES
    chunk_max = s[:, :LANES]
    for c in range(1, n_chunks):
        chunk_max = jnp.maximum(chunk_max, s[:, c * LANES:(c + 1) * LANES])
    m_old = m_ref[idx]
    row_max = jnp.max(chunk_max, axis=1, keepdims=True)
    m_new = jnp.maximum(m_old, row_max if shift is None else row_max + shift)
    m_ref[idx] = m_new
    p = jnp.exp(s - jnp.concatenate([m_new if shift is None else m_new - shift] * n_chunks, axis=1))
    alpha = jnp.exp(m_old - m_new)
    lane_sum = p[:, :LANES]
    for c in range(1, n_chunks):
        lane_sum = lane_sum + p[:, c * LANES:(c + 1) * LANES]
    l_ref[idx] = alpha * l_ref[idx] + lane_sum
    return p, alpha


def _paired_loop(n, step):
    def pair(k, carry):
        step(2 * k)
        step(2 * k + 1)
        return carry

    lax.fori_loop(0, n // 2, pair, 0)

    @pl.when(n % 2 == 1)
    def _():
        step(n - 1)


def _causal_mask(tb):
    return (lax.broadcasted_iota(jnp.int32, (tb, tb), 1) <= lax.broadcasted_iota(jnp.int32, (tb, tb), 0))


def _fox_prompt_kernel(tb, q_ref, k_ref, v_ref, c_ref, ct_ref, o_ref, kbd_ref, vbd_ref, m_ref, l_ref, acc_ref,
                       cq_ref):
    T = q_ref.shape[1]
    units = acc_ref.shape[0]
    lo16 = _head0_mask(BF16, tb)
    lo32 = _head0_mask(F32, tb)
    cols = [slice(u * LANES, (u + 1) * LANES) for u in range(units)]

    def prepare(j, carry):
        ks = pl.ds(pl.multiple_of(j * tb, tb), tb)
        for u in range(units):
            kbd_ref[u, j] = _block_diag_pair(k_ref[0, ks, cols[u]], lo16)
            vbd_ref[u, j] = _block_diag_pair(v_ref[0, ks, cols[u]], lo16)
        return carry

    lax.fori_loop(0, T // tb, prepare, 0)

    def tile(i, j, masked):
        qs = pl.ds(pl.multiple_of(i * tb, tb), tb)
        ks = pl.ds(pl.multiple_of(j * tb, tb), tb)
        for u in range(units):
            z = _dot_nt(q_ref[0, qs, cols[u]], kbd_ref[u, j])
            ps, alphas = [], []
            for h in range(2):
                s = z[:, h * tb:(h + 1) * tb] - ct_ref[0, u, h:h + 1, ks]
                if masked:
                    s = jnp.where(_causal_mask(tb), s, NEG)
                p, alpha = _softmax_step(s, m_ref, l_ref, 2 * u + h, cq_ref[2 * u + h])
                ps.append(p.astype(BF16))
                alphas.append(alpha)
            pv = _dot(jnp.concatenate(ps, axis=1), vbd_ref[u, j])
            acc_ref[u] = jnp.where(lo32, alphas[0], alphas[1]) * acc_ref[u] + pv

    def qblock(i, carry):
        qs = pl.ds(pl.multiple_of(i * tb, tb), tb)
        m_ref[...] = jnp.full_like(m_ref, NEG)
        l_ref[...] = jnp.zeros_like(l_ref)
        acc_ref[...] = jnp.zeros_like(acc_ref)
        for u in range(units):
            for h in range(2):
                cq_ref[2 * u + h] = jnp.broadcast_to(c_ref[0, u, qs, h:h + 1], (tb, LANES))
        tile(i, i, True)
        _paired_loop(i, lambda j: tile(i, j, False))
        for u in range(units):
            denom = jnp.where(lo32, jnp.sum(l_ref[2 * u], axis=1, keepdims=True),
                              jnp.sum(l_ref[2 * u + 1], axis=1, keepdims=True))
            o_ref[0, qs, cols[u]] = (acc_ref[u] / denom).astype(o_ref.dtype)
        return carry

    lax.fori_loop(0, T // tb, qblock, 0)


def _fox_prompt(q, kb, vb, c, ct, first_pair, n_pairs, *, tb, units):
    B, T, _ = q.shape
    assert n_pairs % units == 0 and first_pair % units == 0
    w = units * LANES
    spec = pl.BlockSpec((1, T, w), lambda b, p: (b, 0, first_pair // units + p))
    c4 = c.reshape(B, T, n_pairs, 2).transpose(0, 2, 1, 3)
    ct4 = ct.reshape(B, n_pairs, 2, T)
    nb = T // tb
    return pl.pallas_call(
        functools.partial(_fox_prompt_kernel, tb), grid=(B, n_pairs // units),
        in_specs=[spec] * 3 + [pl.BlockSpec((1, units, T, 2), lambda b, p: (b, p, 0, 0)),
                               pl.BlockSpec((1, units, 2, T), lambda b, p: (b, p, 0, 0))],
        out_specs=pl.BlockSpec((1, T, w), lambda b, p: (b, 0, p)),
        out_shape=jax.ShapeDtypeStruct((B, T, n_pairs * LANES), BF16),
        scratch_shapes=[pltpu.VMEM((units, nb, 2 * tb, LANES), BF16), pltpu.VMEM((units, nb, 2 * tb, LANES), BF16),
                        pltpu.VMEM((2 * units, tb, LANES), F32), pltpu.VMEM((2 * units, tb, LANES), F32),
                        pltpu.VMEM((units, tb, LANES), F32), pltpu.VMEM((2 * units, tb, LANES), F32)],
        compiler_params=_cparams("parallel", "parallel"), name="fox_prompt",
    )(q, kb, vb, c4, ct4)


def _out_proj_kernel(n_parts, x_ref, *refs):
    o_ref = refs[-1]
    y = x_ref[...]
    for i in range(n_parts):
        y = y + _dot(refs[i][...], refs[n_parts + i][...])
    o_ref[...] = y


def _out_proj(x2, parts, w_out, *, tm):
    M, D = x2.shape
    ws, off = [], 0
    for p in parts:
        ws.append(w_out[off:off + p.shape[1]].astype(BF16))
        off += p.shape[1]
    n = len(parts)
    return pl.pallas_call(
        functools.partial(_out_proj_kernel, n), grid=(M // tm,),
        in_specs=[pl.BlockSpec((tm, D), lambda i: (i, 0))]
                 + [pl.BlockSpec((tm, p.shape[1]), lambda i: (i, 0)) for p in parts]
                 + [pl.BlockSpec(w.shape, lambda i: (0, 0)) for w in ws],
        out_specs=pl.BlockSpec((tm, D), lambda i: (i, 0)),
        out_shape=jax.ShapeDtypeStruct((M, D), F32),
        compiler_params=_cparams("parallel"), name="out_proj",
    )(x2, *parts, *ws)


FF_CHUNK = MXU_DIM
HALO = 2 * SUBLANES


def _silu(x):
    return x * (1.0 / (1.0 + jnp.exp(-x)))


def _ffn_chunk(cs, hcat, hb, g_prev, wg_ref, wu_ref, cw_ref, cb_ref, gs_ref, act_ref):
    tm = hb.shape[0]
    gx = _dot(hcat, wg_ref[:, cs])
    gs_ref[...] = gx
    g0 = gx[gx.shape[0] - tm:]
    g1 = g_prev(1, gs_ref[HALO - 1:HALO - 1 + tm, :])
    g2 = g_prev(2, gs_ref[HALO - 2:HALO - 2 + tm, :])
    gc = cb_ref[:, cs] + cw_ref[0:1, cs] * g2
    gc = gc + cw_ref[1:2, cs] * g1
    gc = gc + cw_ref[2:3, cs] * g0
    act_ref[:, cs] = (_silu(gc) * _dot(hb, wu_ref[:, cs])).astype(BF16)
    return g0


def _ffn_prompt_kernel(n_parts, x_ref, halo_ref, *refs):
    parts, part_halos, w_outs = refs[:n_parts], refs[n_parts:2 * n_parts], refs[2 * n_parts:3 * n_parts]
    g_ref, wg_ref, wu_ref, cw_ref, cb_ref, wd_ref, o_ref, conv_ref, gs_ref, act_ref = refs[3 * n_parts:]
    t = pl.program_id(1)
    tm = x_ref.shape[1]
    xcat = jnp.concatenate([halo_ref[0], x_ref[0]], axis=0)
    for i in range(n_parts):
        xcat = xcat + _dot(jnp.concatenate([part_halos[i][0], parts[i][0]], axis=0), w_outs[i][...])
    x = xcat[HALO:]
    hn = _rmsnorm(xcat, g_ref[...])
    hb = hn[HALO:].astype(BF16)
    hh = jnp.where(t == 0, 0.0, hn[:HALO]).astype(BF16)
    hcat = jnp.concatenate([hh, hb], axis=0)
    last = t == pl.num_programs(1) - 1
    for c in range(wg_ref.shape[1] // FF_CHUNK):
        cs = slice(c * FF_CHUNK, (c + 1) * FF_CHUNK)
        g0 = _ffn_chunk(cs, hcat, hb, lambda k, raw: raw, wg_ref, wu_ref, cw_ref, cb_ref, gs_ref, act_ref)

        @pl.when(last)
        def _():
            conv_ref[0, :, cs] = g0[tm - (CONV_W - 1):, :]
    o_ref[0] = x + _dot(act_ref[...], wd_ref[...])


def _ffn_sample_kernel(seq, x_ref, p1_ref, p2_ref, g_ref, wg_ref, wu_ref, cw_ref, cb_ref, wd_ref, o_ref, gate_ref,
                       gs_ref, act_ref):
    x = x_ref[...]
    tm = x.shape[0]
    hb = _rmsnorm(x, g_ref[...]).astype(BF16)
    hcat = jnp.concatenate([jnp.zeros((HALO, x.shape[1]), BF16), hb], axis=0)
    pos = lax.broadcasted_iota(jnp.int32, (tm, FF_CHUNK), 0) % seq
    for c in range(wg_ref.shape[1] // FF_CHUNK):
        cs = slice(c * FF_CHUNK, (c + 1) * FF_CHUNK)
        prev = lambda k, raw: jnp.where(pos < k, (p1_ref, p2_ref)[k - 1][:, cs], raw)
        gate_ref[:, cs] = _ffn_chunk(cs, hcat, hb, prev, wg_ref, wu_ref, cw_ref, cb_ref, gs_ref, act_ref)
    o_ref[...] = x + _dot(act_ref[...], wd_ref[...])


def _resident(a):
    zeros = (0,) * a.ndim
    return pl.BlockSpec(a.shape, lambda *_: zeros, pipeline_mode=pl.Buffered(1))


def _ffn_weights(g, w_gate, w_up, conv_w, conv_b, w_down):
    return (g.reshape(1, -1), w_gate.astype(BF16), w_up.astype(BF16), conv_w, conv_b.reshape(1, -1),
            w_down.astype(BF16))


def _ffn_prompt(x, parts, w_out, g, w_gate, w_up, conv_w, conv_b, w_down, *, tm):
    B, T, D = x.shape
    dff = w_gate.shape[1]
    ws = _ffn_weights(g, w_gate, w_up, conv_w, conv_b, w_down)
    w_outs, off = [], 0
    for p in parts:
        w_outs.append(w_out[off:off + p.shape[2]].astype(BF16))
        off += p.shape[2]
    hpt = tm // HALO
    main = lambda w: pl.BlockSpec((1, tm, w), lambda b, t: (b, t, 0))
    halo = lambda w: pl.BlockSpec((1, HALO, w), lambda b, t: (b, jnp.maximum(t * hpt - 1, 0), 0))
    return pl.pallas_call(
        functools.partial(_ffn_prompt_kernel, len(parts)), grid=(B, T // tm),
        in_specs=[main(D), halo(D)] + [main(p.shape[2]) for p in parts] + [halo(p.shape[2]) for p in parts]
                 + [_resident(w) for w in w_outs] + [_resident(w) for w in ws],
        out_specs=[pl.BlockSpec((1, tm, D), lambda b, t: (b, t, 0)),
                   pl.BlockSpec((1, CONV_W - 1, dff), lambda b, t: (b, 0, 0))],
        out_shape=[jax.ShapeDtypeStruct((B, T, D), F32), jax.ShapeDtypeStruct((B, CONV_W - 1, dff), F32)],
        scratch_shapes=[pltpu.VMEM((tm + HALO, FF_CHUNK), F32), pltpu.VMEM((tm, dff), BF16)],
        compiler_params=_cparams("parallel", "arbitrary"), name="ffn_prompt",
    )(x, x, *parts, *parts, *w_outs, *ws)


def _ffn_sample(x, state, g, w_gate, w_up, conv_w, conv_b, w_down):
    B, T, D = x.shape
    dff = w_gate.shape[1]
    ws = _ffn_weights(g, w_gate, w_up, conv_w, conv_b, w_down)
    zeros = jnp.zeros((B, T, dff), F32)
    p1 = zeros.at[:, 0].set(state[:, 1]).reshape(B * T, dff)
    p2 = zeros.at[:, 0].set(state[:, 0]).at[:, 1].set(state[:, 1]).reshape(B * T, dff)
    args = (x.reshape(B * T, D), p1, p2) + ws
    y, gate = pl.pallas_call(
        functools.partial(_ffn_sample_kernel, T), grid=(1,),
        in_specs=[_resident(a) for a in args],
        out_specs=[pl.BlockSpec((B * T, D), lambda i: (0, 0)), pl.BlockSpec((B * T, dff), lambda i: (0, 0))],
        out_shape=[jax.ShapeDtypeStruct((B * T, D), F32), jax.ShapeDtypeStruct((B * T, dff), F32)],
        scratch_shapes=[pltpu.VMEM((B * T + HALO, FF_CHUNK), F32), pltpu.VMEM((B * T, dff), BF16)],
        compiler_params=_cparams("arbitrary"), name="ffn_sample",
    )(*args)
    return y.reshape(B, T, D), gate.reshape(B, T, dff)[:, T - (CONV_W - 1):]


def _proj_diff_kernel(x_ref, g_ref, w_ref, qn_ref, kn_ref, gmat_ref, q_ref, k_ref, v_ref, kb_ref, vb_ref):
    wq = q_ref.shape[2]
    hb = _rmsnorm(x_ref[0], g_ref[...]).astype(BF16)
    gmat = gmat_ref[...]
    q_ref[0] = (_head_rmsnorm(_dot(hb, w_ref[:, :wq]), gmat, qn_ref[...]) * QK_SCALE).astype(BF16)
    k = _head_rmsnorm(_dot(hb, w_ref[:, wq:2 * wq]), gmat, kn_ref[...])
    k_ref[0] = k
    kb_ref[0] = k.astype(BF16)
    v = _dot(hb, w_ref[:, 2 * wq:])
    v_ref[0] = v
    vb_ref[0] = v.astype(BF16)


def _proj_diff(x, g, w_in, qn, kn, *, tm):
    B, T, D = x.shape
    wv = w_in.shape[1] // 3
    wq = wv
    reps = wq // HEAD_DIM
    args = (x, g.reshape(1, D), w_in.astype(BF16), jnp.tile(qn, reps).reshape(1, wq),
            jnp.tile(kn, reps).reshape(1, wq), _head_mean_matrix())
    full = lambda a: pl.BlockSpec(a.shape, lambda b, t: (0,) * a.ndim)
    row = lambda w: pl.BlockSpec((1, tm, w), lambda b, t: (b, t, 0))
    sds = lambda w, dt: jax.ShapeDtypeStruct((B, T, w), dt)
    return pl.pallas_call(
        _proj_diff_kernel, grid=(B, T // tm),
        in_specs=[row(D)] + [full(a) for a in args[1:]],
        out_specs=[row(wq), row(wq), row(wv), row(wq), row(wv)],
        out_shape=[sds(wq, BF16), sds(wq, F32), sds(wv, F32), sds(wq, BF16), sds(wv, BF16)],
        compiler_params=_cparams("parallel", "parallel"), name="proj_diff",
    )(*args)


def _t5_bucket(rel):
    rel = np.asarray(rel)
    max_exact = N_BUCKETS // 2
    relf = np.maximum(rel, 1).astype(np.float32)
    large = max_exact + (np.log(relf / np.float32(max_exact)) / np.float32(math.log(MAX_DISTANCE / max_exact))
                         * np.float32(N_BUCKETS - max_exact)).astype(np.int32)
    large = np.minimum(large, N_BUCKETS - 1)
    return np.where(rel < max_exact, rel, large).astype(np.int32)


def _near_buckets(q_pos):
    r = np.asarray(q_pos)[:, None]
    c = np.arange(BLK)[None, :]
    assert _t5_bucket(2 * BLK - (BLK - 1)) == N_BUCKETS - 1
    return jnp.asarray(np.stack([_t5_bucket(np.maximum(r - c, 0)), _t5_bucket(BLK + r - c)]))


def _diff_lambda(lq1_ref, lk1_ref, lq2_ref, lk2_ref, lam_init):
    s1 = jnp.sum(lq1_ref[...] * lk1_ref[...], axis=1, keepdims=True)
    s2 = jnp.sum(lq2_ref[...] * lk2_ref[...], axis=1, keepdims=True)
    return jnp.exp(s1) - jnp.exp(s2) + lam_init


def _diff_prompt_kernel(lam_init, rb_ref, q_ref, k_ref, v_ref, bk_ref, lq1_ref, lk1_ref, lq2_ref, lk2_ref,
                        sub_ref, o_ref, m_ref, l_ref, acc_ref, bias_ref, kbd_ref, vbd_ref):
    T = q_ref.shape[1]
    units, _, tb, _ = bias_ref.shape
    nsub = tb // BLK
    lo16 = _head0_mask(BF16, tb)
    lam = _diff_lambda(lq1_ref, lk1_ref, lq2_ref, lk2_ref, lam_init)
    cols = [slice(u * LANES, (u + 1) * LANES) for u in range(units)]
    heads = [pl.program_id(1) * units + u for u in range(units)]
    far_bias = [rb_ref[N_BUCKETS - 1, h] for h in heads]

    for u in range(units):
        near = []
        for d in range(2):
            tile_bias = jnp.zeros((BLK, BLK), F32)
            for b in range(N_BUCKETS):
                tile_bias = jnp.where(bk_ref[d] == b, rb_ref[b, heads[u]], tile_bias)
            near.append(tile_bias)
        bias_ref[u] = jnp.full(bias_ref.shape[1:], far_bias[u], F32)
        for a in range(nsub):
            bias_ref[u, 0, a * BLK:(a + 1) * BLK, a * BLK:(a + 1) * BLK] = near[0]
            if a >= 1:
                bias_ref[u, 0, a * BLK:(a + 1) * BLK, (a - 1) * BLK:a * BLK] = near[1]
        bias_ref[u, 1, 0:BLK, (nsub - 1) * BLK:nsub * BLK] = near[1]

    def prepare(j, carry):
        ks = pl.ds(pl.multiple_of(j * tb, tb), tb)
        for u in range(units):
            kbd_ref[u, j] = _block_diag_pair(k_ref[0, ks, cols[u]], lo16)
            vb = v_ref[0, ks, cols[u]]
            zero = jnp.zeros_like(vb)
            vbd_ref[u, j] = jnp.concatenate([jnp.concatenate([vb, zero], axis=1),
                                             jnp.concatenate([zero, vb], axis=1)], axis=0)
        return carry

    lax.fori_loop(0, T // tb, prepare, 0)

    def tile(i, j, near, masked):
        qs = pl.ds(pl.multiple_of(i * tb, tb), tb)
        for u in range(units):
            z = _dot_nt(q_ref[0, qs, cols[u]], kbd_ref[u, j])
            ps, alphas = [], []
            for mp in range(2):
                s = z[:, mp * tb:(mp + 1) * tb]
                if near is not None:
                    s = s + bias_ref[u, near]
                if masked:
                    s = jnp.where(_causal_mask(tb), s, NEG)
                p, alpha = _softmax_step(s, m_ref, l_ref, 2 * u + mp, None if near is not None else far_bias[u])
                ps.append(p.astype(BF16))
                alphas.append(alpha)
            pv = _dot(jnp.concatenate(ps, axis=1), vbd_ref[u, j])
            acc_ref[u] = jnp.concatenate(alphas, axis=1) * acc_ref[u] + pv

    def qblock(i, carry):
        qs = pl.ds(pl.multiple_of(i * tb, tb), tb)
        m_ref[...] = jnp.full_like(m_ref, NEG)
        l_ref[...] = jnp.zeros_like(l_ref)
        acc_ref[...] = jnp.zeros_like(acc_ref)
        tile(i, i, 0, True)

        @pl.when(i >= 1)
        def _():
            tile(i, i - 1, 1, False)

        _paired_loop(jnp.maximum(i - 1, 0), lambda j: tile(i, j, None, False))
        for u in range(units):
            o_map = [acc_ref[u, :, mp * LANES:(mp + 1) * LANES] / jnp.sum(l_ref[2 * u + mp], axis=1, keepdims=True)
                     for mp in range(2)]
            o = o_map[0] - lam * o_map[1]
            o_ref[0, qs, cols[u]] = (_rmsnorm(o, sub_ref[...]) * (1.0 - lam_init)).astype(o_ref.dtype)
        return carry

    lax.fori_loop(0, T // tb, qblock, 0)


def _diff_prompt(q, kb, vb, rel_bias, lq1, lk1, lq2, lk2, subln, lam_init, *, tb, units):
    B, T, W = q.shape
    n_heads = W // LANES
    assert n_heads % units == 0
    buckets = _near_buckets(np.arange(BLK))
    spec = pl.BlockSpec((1, T, units * LANES), lambda b, h: (b, 0, h))
    small = lambda a: pl.BlockSpec(a.shape, lambda b, h: (0,) * a.ndim)
    vecs = [a.reshape(1, -1) for a in (lq1, lk1, lq2, lk2, subln)]
    nb = T // tb
    return pl.pallas_call(
        functools.partial(_diff_prompt_kernel, lam_init), grid=(B, n_heads // units),
        in_specs=[pl.BlockSpec(memory_space=pltpu.SMEM)] + [spec] * 3 + [small(buckets)] + [small(a) for a in vecs],
        out_specs=spec,
        scratch_shapes=[pltpu.VMEM((2 * units, tb, LANES), F32), pltpu.VMEM((2 * units, tb, LANES), F32),
                        pltpu.VMEM((units, tb, 2 * LANES), F32), pltpu.VMEM((units, 2, tb, tb), F32),
                        pltpu.VMEM((units, nb, 2 * tb, LANES), BF16),
                        pltpu.VMEM((units, nb, 2 * tb, 2 * LANES), BF16)],
        out_shape=jax.ShapeDtypeStruct((B, T, W), BF16),
        compiler_params=_cparams("parallel", "parallel"), name="diff_prompt",
    )(rel_bias, q, kb, vb, buckets, *vecs)


GROUPS_PER_CHUNK = MXU_DIM // HEAD_DIM


def _chunked_queries(q):
    B, T, W = q.shape
    qg = q.reshape(B, T, W // MXU_DIM, GROUPS_PER_CHUNK, HEAD_DIM).transpose(0, 2, 3, 1, 4)
    eye = jnp.eye(GROUPS_PER_CHUNK, dtype=q.dtype)
    out = qg[:, :, :, :, None, :] * eye[None, None, :, None, :, None]
    return out.reshape(B, W // MXU_DIM, GROUPS_PER_CHUNK * T, MXU_DIM)


def _keys_on_lanes(cache):
    nd = cache.ndim
    t = cache.transpose((0, 1) + tuple(range(3, nd)) + (2,))
    return t.reshape(cache.shape[0], cache.shape[1], -1, cache.shape[2])


def _scores(qc_ref, key_tiles, transposed):
    rows = []
    for c in range(qc_ref.shape[1]):
        cs = slice(c * MXU_DIM, (c + 1) * MXU_DIM)
        if transposed:
            rows.append(_dot(qc_ref[0, c], jnp.concatenate([kt[cs, :].astype(BF16) for kt in key_tiles], axis=1)))
        else:
            rows.append(_dot_nt(qc_ref[0, c], jnp.concatenate([kt[:, cs].astype(BF16) for kt in key_tiles], axis=0)))
    return jnp.concatenate(rows, axis=0)


def _mix_sample_kernel(seq, n_sb, n_pages, pt_ref, qc_ref, kn_ref, vn_ref, lfn_ref, *refs):
    kp = refs[:n_pages]
    vp = refs[n_pages:2 * n_pages]
    lfp = refs[2 * n_pages:3 * n_pages]
    o_ref, run_ref, m_ref, l_ref, acc_ref = refs[3 * n_pages:]
    j = pl.program_id(1)
    sb = n_sb * seq
    n_fox = (BLK - sb) // seq
    rpc = GROUPS_PER_CHUNK * seq
    n_chunks = qc_ref.shape[1]
    suffix = _suffix_matrix()

    def block(z, lfts, pv_of, mask):
        n = len(lfts)
        zs, zf = z[:sb], z[sb:]
        ls = _log_sigmoid_scores(zs)
        l1m = ls - zs
        a_parts, s_parts = [], []
        for g in range(n):
            gs = slice(g * BLK, (g + 1) * BLK)
            elf = jnp.concatenate([jnp.broadcast_to(lfts[g][h:h + 1, :], (seq, BLK)) for h in range(n_fox)], axis=0)
            x = jnp.concatenate([l1m[:, gs], elf], axis=0)
            if mask is not None:
                x = jnp.where(mask, x, 0.0)
            cs = _dot(_split(x), suffix)
            run = run_ref[...]
            run_ref[...] = run + cs[:, BLK:]
            later = cs[:, :BLK] + run
            a = jnp.exp(ls[:, gs] + later[:sb])
            s = zf[:, gs] + later[sb:]
            if mask is not None:
                a = jnp.where(mask[:sb], a, 0.0)
                s = jnp.where(mask[sb:], s, NEG)
            a_parts.append(a)
            s_parts.append(s)
        p, alpha = _softmax_step(jnp.concatenate(s_parts, axis=1), m_ref, l_ref, 0)
        w = jnp.concatenate([jnp.concatenate(a_parts, axis=1), p], axis=0).astype(BF16)
        for c in range(n_chunks):
            pv = pv_of(c, w[c * rpc:(c + 1) * rpc])
            if (c + 1) * rpc <= sb:
                acc_ref[c] += pv
            else:
                al = alpha[c * rpc - sb:(c + 1) * rpc - sb]
                acc_ref[c] = jnp.concatenate([al] * (MXU_DIM // LANES), axis=1) * acc_ref[c] + pv

    @pl.when(j == 0)
    def _():
        run_ref[...] = jnp.zeros_like(run_ref)
        m_ref[...] = jnp.full_like(m_ref, NEG)
        l_ref[...] = jnp.zeros_like(l_ref)
        acc_ref[...] = jnp.zeros_like(acc_ref)
        lane = lax.broadcasted_iota(jnp.int32, (BLK, BLK), 1)
        row = lax.broadcasted_iota(jnp.int32, (BLK, BLK), 0)
        own = lane < row % seq + jnp.where(row < sb, 0, 1)
        kn = _pad_rows(kn_ref[0], BLK)
        vn = _pad_rows(vn_ref[0], BLK).astype(BF16)
        block(_scores(qc_ref, [kn], False), [lfn_ref[0]],
              lambda c, w: _dot(w, vn[:, c * MXU_DIM:(c + 1) * MXU_DIM]), own)

    def page_values(c, w):
        cs = slice(c * MXU_DIM, (c + 1) * MXU_DIM)
        return _dot_nt(w, jnp.concatenate([v[cs, :].astype(BF16) for v in vp], axis=1))

    block(_scores(qc_ref, kp, True), [r[...] for r in lfp], page_values, None)

    @pl.when(j == pl.num_programs(1) - 1)
    def _():
        col_group = lax.broadcasted_iota(jnp.int32, (seq, MXU_DIM), 1) // HEAD_DIM
        denom = jnp.sum(l_ref[0], axis=1, keepdims=True)
        outs = []
        for c in range(n_chunks):
            rows = acc_ref[c]
            if (c + 1) * rpc > sb:
                rows = rows / denom[c * rpc - sb:(c + 1) * rpc - sb]
            out = jnp.zeros((seq, MXU_DIM), F32)
            for g in range(GROUPS_PER_CHUNK):
                out = jnp.where(col_group == g, rows[g * seq:(g + 1) * seq, :], out)
            outs.append(out)
        o_ref[0] = jnp.concatenate(outs, axis=1)


def _paged_specs(shape_tail, layer, n_total, per_step):
    def spec(g):
        zeros = (0,) * len(shape_tail)
        return pl.BlockSpec((None, None) + shape_tail,
                            lambda b, j, pt: (layer, pt[b, n_total - 1 - (j * per_step + g)]) + zeros)
    return [spec(g) for g in range(per_step)]


def _mix_sample(q, k_new, v_new, logf_new, cache_k, cache_v, cache_logf, layer, page_table, n_sb):
    B, T, W = q.shape
    n_heads = W // HEAD_DIM
    n_fox = n_heads - n_sb
    n_total = page_table.shape[1]
    g = min(PAGES_PER_STEP, n_total)
    assert n_heads * T == BLK and cache_k.shape[2] == BLK and n_total % g == 0
    assert (n_sb * T) % (GROUPS_PER_CHUNK * T) == 0
    lfn = jnp.zeros((B, n_fox, BLK), F32).at[:, :, :T].set(logf_new.transpose(0, 2, 1))
    args = (_chunked_queries(q), k_new, v_new, lfn)
    per_seq = lambda a: pl.BlockSpec((1,) + a.shape[1:], lambda b, j, pt: (b,) + (0,) * (a.ndim - 1))
    return pl.pallas_call(
        functools.partial(_mix_sample_kernel, T, n_sb, g),
        grid_spec=pltpu.PrefetchScalarGridSpec(
            num_scalar_prefetch=1, grid=(B, n_total // g),
            in_specs=[per_seq(a) for a in args] + _paged_specs((W, BLK), layer, n_total, g) * 2
                     + _paged_specs((n_fox, BLK), layer, n_total, g),
            out_specs=pl.BlockSpec((1, T, W), lambda b, j, pt: (b, 0, 0)),
            scratch_shapes=[pltpu.VMEM((BLK, BLK), F32), pltpu.VMEM((1, n_fox * T, BLK), F32),
                            pltpu.VMEM((1, n_fox * T, BLK), F32),
                            pltpu.VMEM((W // MXU_DIM, GROUPS_PER_CHUNK * T, MXU_DIM), F32)]),
        out_shape=jax.ShapeDtypeStruct((B, T, W), F32),
        compiler_params=_cparams("parallel", "arbitrary"), name="mix_sample",
    )(page_table, *args, *([_keys_on_lanes(cache_k)] * g), *([_keys_on_lanes(cache_v)] * g),
      *([_keys_on_lanes(cache_logf)] * g))


def _diff_sample_kernel(seq, lam_init, n_pages, pt_ref, qc_ref, kn_ref, vn_ref, bk_ref, rbr_ref,
                        lq1_ref, lk1_ref, lq2_ref, lk2_ref, sub_ref, *refs):
    kp = refs[:n_pages]
    vp = refs[n_pages:2 * n_pages]
    o_ref, m_ref, l_ref, acc_ref, bias_ref = refs[2 * n_pages:]
    j = pl.program_id(1)
    n_heads = vn_ref.shape[2] // BLK
    rph = 2 * seq

    def near_bias(d):
        out = jnp.zeros((BLK, BLK), F32)
        for b in range(N_BUCKETS):
            out = jnp.where(bk_ref[d] == b, rbr_ref[:, b:b + 1], out)
        return out

    def block(s, values_of, mask):
        if mask is not None:
            s = jnp.where(mask, s, NEG)
        p, alpha = _softmax_step(s, m_ref, l_ref, 0)
        w = p.astype(BF16)
        pv = jnp.concatenate([_dot(w[h * rph:(h + 1) * rph], values_of(h)) for h in range(n_heads)], axis=0)
        acc_ref[...] = alpha * acc_ref[...] + pv

    far = jnp.broadcast_to(rbr_ref[:, N_BUCKETS - 1:N_BUCKETS], (BLK, BLK))

    @pl.when(j == 0)
    def _():
        m_ref[...] = jnp.full_like(m_ref, NEG)
        l_ref[...] = jnp.zeros_like(l_ref)
        acc_ref[...] = jnp.zeros_like(acc_ref)
        lane = lax.broadcasted_iota(jnp.int32, (BLK, BLK), 1)
        row = lax.broadcasted_iota(jnp.int32, (BLK, BLK), 0)
        kn = _pad_rows(kn_ref[0], BLK)
        block(_scores(qc_ref, [kn], False) + near_bias(0),
              lambda h: _pad_rows(vn_ref[0, :, h * BLK:(h + 1) * BLK], BLK).astype(BF16), lane <= row % seq)
        bias_ref[...] = jnp.concatenate([near_bias(1)] + [far] * (n_pages - 1), axis=1)

    @pl.when(j == 1)
    def _():
        bias_ref[...] = jnp.concatenate([far] * n_pages, axis=1)

    block(_scores(qc_ref, kp, True) + bias_ref[...],
          lambda h: jnp.concatenate([v[pl.ds(h, BLK, stride=n_heads), :].astype(BF16) for v in vp], axis=0), None)

    @pl.when(j == pl.num_programs(1) - 1)
    def _():
        rows = acc_ref[...] / jnp.sum(l_ref[0], axis=1, keepdims=True)
        lam = _diff_lambda(lq1_ref, lk1_ref, lq2_ref, lk2_ref, lam_init)
        sub = sub_ref[...]
        outs = []
        for h in range(n_heads):
            o = rows[h * rph:h * rph + seq] - lam * rows[h * rph + seq:(h + 1) * rph]
            outs.append(_rmsnorm(o, sub))
        o_ref[0] = jnp.concatenate(outs, axis=1) * (1.0 - lam_init)


def _diff_sample(q, k_new, v_new, cache_k, cache_v, layer, page_table, rel_bias, lq1, lk1, lq2, lk2, subln, lam_init):
    B, T, W = q.shape
    n_heads = cache_v.shape[3]
    n_total = page_table.shape[1]
    g = min(PAGES_PER_STEP, n_total)
    assert (W // HEAD_DIM) * T == BLK and cache_k.shape[2] == BLK and n_total % g == 0
    assert cache_v.shape[4] == BLK and v_new.shape[2] == n_heads * BLK
    L, P = cache_v.shape[:2]
    v_rows = cache_v.reshape(L, P, BLK * n_heads, BLK)
    buckets = _near_buckets(np.arange(BLK) % T)
    rb_rows = jnp.repeat(rel_bias.T, 2 * T, axis=0)
    vecs = [a.reshape(1, -1) for a in (lq1, lk1, lq2, lk2, subln)]
    args = (_chunked_queries(q), k_new, v_new)
    per_seq = lambda a: pl.BlockSpec((1,) + a.shape[1:], lambda b, j, pt: (b,) + (0,) * (a.ndim - 1))
    small = lambda a: pl.BlockSpec(a.shape, lambda b, j, pt: (0,) * a.ndim)
    return pl.pallas_call(
        functools.partial(_diff_sample_kernel, T, lam_init, g),
        grid_spec=pltpu.PrefetchScalarGridSpec(
            num_scalar_prefetch=1, grid=(B, n_total // g),
            in_specs=[per_seq(a) for a in args] + [small(buckets), small(rb_rows)] + [small(a) for a in vecs]
                     + _paged_specs((W, BLK), layer, n_total, g) + _paged_specs((BLK * n_heads, BLK), layer, n_total, g),
            out_specs=pl.BlockSpec((1, T, n_heads * BLK), lambda b, j, pt: (b, 0, 0)),
            scratch_shapes=[pltpu.VMEM((1, BLK, BLK), F32), pltpu.VMEM((1, BLK, BLK), F32),
                            pltpu.VMEM((BLK, BLK), F32), pltpu.VMEM((BLK, g * BLK), F32)]),
        out_shape=jax.ShapeDtypeStruct((B, T, n_heads * BLK), F32),
        compiler_params=_cparams("parallel", "arbitrary"), name="diff_sample",
    )(page_table, *args, buckets, rb_rows, *vecs, *([_keys_on_lanes(cache_k)] * g), *([v_rows] * g))


def kernel(x_prompt, x_sample, cache_k_mix, cache_v_mix, cache_logf_mix, cache_k_diff, cache_v_diff, state_ffn_conv, page_table, rel_bias, norm_mix, w_in_mix, b_forget, qnorm_fox, knorm_fox, w_out_mix, norm_diff, w_in_diff, qnorm_diff, knorm_diff, lambda_q1, lambda_k1, lambda_q2, lambda_k2, subln_diff, w_out_diff, norm_ffn, w_gate, w_up, conv_w, conv_b, w_down):
    B, T, D = x_prompt.shape
    DB, DT, _ = x_sample.shape
    depth = norm_ffn.shape[0]
    n_fox = b_forget.shape[1]
    n_mix = cache_k_mix.shape[3]
    n_sb = n_mix - n_fox
    n_diff = cache_k_diff.shape[3]
    sb_pairs = n_sb * HEAD_DIM // LANES
    fox_pairs = n_fox * HEAD_DIM // LANES
    tm = min(T, PROMPT_ROW_TILE)
    tb = min(T, ATT_BLOCK)
    xp, xs = x_prompt, x_sample
    outs = {n: [] for n in ("kmp", "vmp", "lfp", "kms", "vms", "lfs", "kdp", "vdp", "kds", "vds", "cp", "cs")}
    for l in range(depth):
        j = l // 2
        if l % 2 == 0:
            pw = (norm_mix[j], w_in_mix[j], b_forget[j], qnorm_fox[j], knorm_fox[j])
            q, k, v, kb, vb, logf, c, ct = _proj_mix(xp, *pw, tm=tm, with_cumsum=True)
            o_sb = _sb_prompt(q, kb, vb, sb_pairs, group=min(T // BLK, SB_GROUP))
            o_fx = _fox_prompt(q, kb, vb, c, ct, sb_pairs, fox_pairs, tb=tb, units=ATT_UNITS)
            mixed, w_out = [o_sb, o_fx], w_out_mix[j]
            outs["kmp"].append(k.reshape(B, T, n_mix, HEAD_DIM))
            outs["vmp"].append(v.reshape(B, T, n_mix, HEAD_DIM))
            outs["lfp"].append(logf)
            qs, ks, vs, _, _, lfs = _proj_mix(xs.reshape(1, DB * DT, D), *pw, tm=DB * DT, with_cumsum=False)
            ks, vs, lfs = ks.reshape(DB, DT, -1), vs.reshape(DB, DT, -1), lfs.reshape(DB, DT, n_fox)
            o = _mix_sample(qs.reshape(DB, DT, -1), ks, vs, lfs, cache_k_mix, cache_v_mix, cache_logf_mix, j,
                            page_table, n_sb)
            xs = _out_proj(xs.reshape(DB * DT, D), [o.reshape(DB * DT, -1).astype(BF16)], w_out_mix[j],
                           tm=DB * DT).reshape(DB, DT, D)
            outs["kms"].append(ks.reshape(DB, DT, n_mix, HEAD_DIM))
            outs["vms"].append(vs.reshape(DB, DT, n_mix, HEAD_DIM))
            outs["lfs"].append(lfs)
        else:
            lam_init = 0.8 - 0.6 * math.exp(-0.3 * l)
            pw = (norm_diff[j], w_in_diff[j], qnorm_diff[j], knorm_diff[j])
            lam_w = (lambda_q1[j], lambda_k1[j], lambda_q2[j], lambda_k2[j], subln_diff[j], lam_init)
            q, k, v, kb, vb = _proj_diff(xp, *pw, tm=tm)
            o = _diff_prompt(q, kb, vb, rel_bias, *lam_w, tb=tb, units=ATT_UNITS)
            mixed, w_out = [o], w_out_diff[j]
            outs["kdp"].append(k.reshape(B, T, n_diff, 2, HEAD_DIM))
            outs["vdp"].append(v.reshape(B, T, n_diff, 2 * HEAD_DIM))
            qs, ks, vs, _, _ = _proj_diff(xs.reshape(1, DB * DT, D), *pw, tm=DB * DT)
            ks, vs = ks.reshape(DB, DT, -1), vs.reshape(DB, DT, -1)
            o = _diff_sample(qs.reshape(DB, DT, -1), ks, vs, cache_k_diff, cache_v_diff, j, page_table, rel_bias,
                             *lam_w)
            xs = _out_proj(xs.reshape(DB * DT, D), [o.reshape(DB * DT, -1).astype(BF16)], w_out_diff[j],
                           tm=DB * DT).reshape(DB, DT, D)
            outs["kds"].append(ks.reshape(DB, DT, n_diff, 2, HEAD_DIM))
            outs["vds"].append(vs.reshape(DB, DT, n_diff, 2 * HEAD_DIM))
        fw = (norm_ffn[l], w_gate[l], w_up[l], conv_w[l], conv_b[l], w_down[l])
        xp, cp = _ffn_prompt(xp, mixed, w_out, *fw, tm=min(T, FFN_ROW_TILE))
        xs, cs = _ffn_sample(xs, state_ffn_conv[l], *fw)
        outs["cp"].append(cp)
        outs["cs"].append(cs)
    st = {n: jnp.stack(a) for n, a in outs.items()}
    return (xp, xs, st["kmp"], st["vmp"], st["lfp"], st["kms"], st["vms"], st["lfs"], st["kdp"], st["vdp"],
            st["kds"], st["vds"], st["cp"], st["cs"])
```

```python
import functools
import math

import numpy as np
import jax
import jax.numpy as jnp
from jax import lax
from jax.experimental import pallas as pl
from jax.experimental.pallas import tpu as pltpu

F32 = jnp.float32
BF16 = jnp.bfloat16

HEAD_DIM = 64
N_BUCKETS = 32
MAX_DISTANCE = 128
CONV_W = 3
EPS = 1e-6
QK_SCALE = HEAD_DIM ** -0.5

LANES = 128
SUBLANES = 8
MXU_DIM = 256
VMEM_LIMIT = 56 * 1024 * 1024

BLK = LANES
NEG = -1e30
EXP_ZERO_BELOW = -104.0

PROMPT_ROW_TILE = 512
FFN_ROW_TILE = 1024
ATT_BLOCK = 512
ATT_UNITS = 2
SB_GROUP = 8
PAGES_PER_STEP = 16


def _cparams(*sem):
    return pltpu.CompilerParams(dimension_semantics=sem, vmem_limit_bytes=VMEM_LIMIT)


def _dot(a, b):
    return jnp.dot(a, b, preferred_element_type=F32)


def _dot_nt(a, b):
    return lax.dot_general(a, b, (((1,), (1,)), ((), ())), preferred_element_type=F32)


def _dot_f32(a, b):
    return jnp.dot(a, b, preferred_element_type=F32, precision=lax.Precision.HIGHEST)


def _split(a):
    hi = a.astype(BF16)
    lo = (a - hi.astype(F32)).astype(BF16)
    return jnp.concatenate([hi, lo], axis=1)


def _log_sigmoid(x):
    return jnp.minimum(x, 0.0) - jnp.log1p(jnp.exp(-jnp.abs(x)))


def _log_sigmoid_scores(x):
    return jnp.minimum(x, 0.0) - jnp.log(1.0 + jnp.exp(-jnp.abs(x)))


def _rmsnorm(x, g):
    return x * lax.rsqrt(jnp.mean(x * x, axis=-1, keepdims=True) + EPS) * g


def _head_rmsnorm(y, gmat2, gain):
    outs = []
    for c in range(y.shape[1] // LANES):
        yc = y[:, c * LANES:(c + 1) * LANES]
        outs.append(yc * lax.rsqrt(_dot(_split(yc * yc), gmat2) + EPS))
    return jnp.concatenate(outs, axis=1) * gain


def _head_mean_matrix():
    idx = np.arange(LANES) // HEAD_DIM
    g = (idx[:, None] == idx[None, :]).astype(np.float32) / HEAD_DIM
    return jnp.asarray(np.concatenate([g, g], axis=0), BF16)


def _suffix_matrix():
    r = lax.broadcasted_iota(jnp.int32, (2 * BLK, 2 * BLK), 0) & (BLK - 1)
    c = lax.broadcasted_iota(jnp.int32, (2 * BLK, 2 * BLK), 1)
    return jnp.where((c >= BLK) | (r > c), 1.0, 0.0).astype(BF16)


def _head0_mask(dtype, rows=BLK):
    lane = lax.broadcasted_iota(jnp.int32, (rows, LANES), 1)
    if dtype == F32:
        return lane < HEAD_DIM
    return jnp.where(lane < HEAD_DIM, 1.0, 0.0).astype(dtype) > 0


def _block_diag_pair(x, lo_half):
    zero = jnp.zeros_like(x)
    return jnp.concatenate([jnp.where(lo_half, x, zero), jnp.where(lo_half, zero, x)], axis=0)


def _pad_rows(x, n):
    return jnp.concatenate([x, jnp.zeros((n - x.shape[0],) + x.shape[1:], x.dtype)], axis=0)


def _proj_mix_kernel(with_cumsum, x_ref, g_ref, w_ref, wf_ref, wft_ref, bfr_ref, bfc_ref, qn_ref, kn_ref,
                     gmat_ref, q_ref, k_ref, v_ref, kb_ref, vb_ref, logf_ref, *rest):
    ws = q_ref.shape[2] // 2
    n_fox = logf_ref.shape[2]
    tm = x_ref.shape[1]
    hb = _rmsnorm(x_ref[0], g_ref[...]).astype(BF16)

    def sec(i):
        return _dot(hb, w_ref[:, i * ws:(i + 1) * ws])

    q_ref[0, :, :ws] = (sec(0) * QK_SCALE).astype(BF16)
    k_sb = sec(1)
    k_ref[0, :, :ws] = k_sb
    kb_ref[0, :, :ws] = k_sb.astype(BF16)
    v_sb = sec(2)
    v_ref[0, :, :ws] = v_sb
    vb_ref[0, :, :ws] = v_sb.astype(BF16)
    gmat = gmat_ref[...]
    q_ref[0, :, ws:] = (_head_rmsnorm(sec(3), gmat, qn_ref[...]) * QK_SCALE).astype(BF16)
    k_fx = _head_rmsnorm(sec(4), gmat, kn_ref[...])
    k_ref[0, :, ws:] = k_fx
    kb_ref[0, :, ws:] = k_fx.astype(BF16)
    v_fx = sec(5)
    v_ref[0, :, ws:] = v_fx
    vb_ref[0, :, ws:] = v_fx.astype(BF16)

    logf = _log_sigmoid(_dot(hb, wf_ref[...]) + bfr_ref[...])
    logf_ref[0] = logf[:, :n_fox]
    if not with_cumsum:
        return
    c_ref, ct_ref, carry_ref, carryt_ref = rest

    @pl.when(pl.program_id(1) == 0)
    def _():
        carry_ref[...] = jnp.zeros_like(carry_ref)
        carryt_ref[...] = jnp.zeros_like(carryt_ref)

    r = lax.broadcasted_iota(jnp.int32, (tm, tm), 0)
    c = lax.broadcasted_iota(jnp.int32, (tm, tm), 1)
    csum = _dot_f32(jnp.where(c <= r, 1.0, 0.0).astype(F32), logf) + carry_ref[...]
    for p in range(n_fox // 2):
        c_ref[0, p] = csum[:, 2 * p:2 * p + 2]
    carry_ref[...] = csum[tm - 1:tm, :]
    logft = _log_sigmoid(_dot_nt(wft_ref[...], hb) + bfc_ref[:, 0:1])
    csumt = _dot_f32(logft, jnp.where(r <= c, 1.0, 0.0).astype(F32)) + carryt_ref[:, 0:1]
    ct_ref[0] = csumt[:n_fox, :]
    carryt_ref[...] = jnp.broadcast_to(csumt[:, tm - 1:tm], carryt_ref.shape)


def _proj_mix(x, g, w_in, b_f, qn, kn, *, tm, with_cumsum):
    B, T, D = x.shape
    n_fox = b_f.shape[0]
    ws = (w_in.shape[1] - n_fox) // 6
    wmain = w_in[:, :6 * ws].astype(BF16)
    wf = jnp.zeros((D, LANES), BF16).at[:, :n_fox].set(w_in[:, 6 * ws:].astype(BF16))
    wft = jnp.zeros((2 * SUBLANES, D), BF16).at[:n_fox, :].set(w_in[:, 6 * ws:].T.astype(BF16))
    bfr = jnp.zeros((1, LANES), F32).at[0, :n_fox].set(b_f)
    bfc = jnp.zeros((2 * SUBLANES, LANES), F32).at[:n_fox, :].set(jnp.broadcast_to(b_f[:, None], (n_fox, LANES)))
    reps = ws // HEAD_DIM
    full = lambda a: pl.BlockSpec(a.shape, lambda b, t: (0,) * a.ndim)
    row = lambda w: pl.BlockSpec((1, tm, w), lambda b, t: (b, t, 0))
    args = (x, g.reshape(1, D), wmain, wf, wft, bfr, bfc, jnp.tile(qn, reps).reshape(1, ws),
            jnp.tile(kn, reps).reshape(1, ws), _head_mean_matrix())
    out_shape = [jax.ShapeDtypeStruct((B, T, 2 * ws), BF16), jax.ShapeDtypeStruct((B, T, 2 * ws), F32),
                 jax.ShapeDtypeStruct((B, T, 2 * ws), F32), jax.ShapeDtypeStruct((B, T, 2 * ws), BF16),
                 jax.ShapeDtypeStruct((B, T, 2 * ws), BF16), jax.ShapeDtypeStruct((B, T, n_fox), F32)]
    out_specs = [row(2 * ws)] * 5 + [row(n_fox)]
    scratch = []
    if with_cumsum:
        out_shape += [jax.ShapeDtypeStruct((B, n_fox // 2, T, 2), F32), jax.ShapeDtypeStruct((B, n_fox, T), F32)]
        out_specs += [pl.BlockSpec((1, n_fox // 2, tm, 2), lambda b, t: (b, 0, t, 0)),
                      pl.BlockSpec((1, n_fox, tm), lambda b, t: (b, 0, t))]
        scratch = [pltpu.VMEM((1, LANES), F32), pltpu.VMEM((2 * SUBLANES, LANES), F32)]
    return pl.pallas_call(
        functools.partial(_proj_mix_kernel, with_cumsum),
        grid=(B, T // tm),
        in_specs=[row(D)] + [full(a) for a in args[1:]],
        out_specs=out_specs, out_shape=out_shape, scratch_shapes=scratch,
        compiler_params=_cparams("parallel", "arbitrary"), name="proj_mix",
    )(*args)


def _sb_prompt_kernel(q_ref, k_ref, v_ref, o_ref, r_ref, acc_ref):
    T = q_ref.shape[1]
    group = r_ref.shape[0]
    lane = lax.broadcasted_iota(jnp.int32, (BLK, BLK), 1)
    row = lax.broadcasted_iota(jnp.int32, (BLK, BLK), 0)
    lo_half = _head0_mask(BF16)
    strict = lane < row
    strict2 = jnp.concatenate([strict, strict], axis=1)
    suffix = _suffix_matrix()

    def tile(a, iq, ik, valid, masked):
        q2 = q_ref[0, pl.ds(pl.multiple_of(iq * BLK, BLK), BLK), :]
        ks = pl.ds(pl.multiple_of(ik * BLK, BLK), BLK)
        z = _dot_nt(q2, _block_diag_pair(k_ref[0, ks, :], lo_half))
        ls = _log_sigmoid_scores(z)
        l1m = ls - z
        if masked:
            l1m = jnp.where(strict2, l1m, 0.0)
        w = []
        for h in range(2):
            hs = slice(h * BLK, (h + 1) * BLK)
            cs = _dot(_split(l1m[:, hs]), suffix)
            run = r_ref[a, :, hs]
            wh = jnp.exp(ls[:, hs] + cs[:, :BLK] + run)
            total = cs[:, BLK:]
            if masked:
                wh = jnp.where(strict, wh, 0.0)
            if valid is not None:
                wh = wh * valid
                total = total * valid
            w.append(wh.astype(BF16))
            r_ref[a, :, hs] = run + total
        acc_ref[a] += _dot(jnp.concatenate(w, axis=1), _block_diag_pair(v_ref[0, ks, :], lo_half))

    def qgroup(i, carry):
        base = i * group
        r_ref[...] = jnp.zeros_like(r_ref)
        acc_ref[...] = jnp.zeros_like(acc_ref)
        for a in range(group):
            tile(a, base + a, base + a, None, True)

        def alive(d):
            worst = jnp.full((BLK, 2 * BLK), NEG, F32)
            for a in range(group):
                worst = jnp.maximum(worst, jnp.where(base + a - d >= 0, r_ref[a], NEG))
            return (jnp.max(worst) > EXP_ZERO_BELOW).astype(jnp.int32)

        def cond(s):
            return s[1] > 0

        def body(s):
            d = s[0]
            for a in range(group):
                ik = base + a - d
                tile(a, base + a, jnp.maximum(ik, 0), (ik >= 0).astype(F32), False)
            return d + 1, alive(d + 1)

        lax.while_loop(cond, body, (1, alive(1)))
        for a in range(group):
            o_ref[0, pl.ds(pl.multiple_of((base + a) * BLK, BLK), BLK), :] = acc_ref[a].astype(o_ref.dtype)
        return carry

    lax.fori_loop(0, T // (group * BLK), qgroup, 0)


def _sb_prompt(q, kb, vb, n_pairs, *, group):
    B, T, _ = q.shape
    assert T % (group * BLK) == 0
    spec = pl.BlockSpec((1, T, LANES), lambda b, p: (b, 0, p))
    return pl.pallas_call(
        _sb_prompt_kernel, grid=(B, n_pairs), in_specs=[spec] * 3, out_specs=spec,
        out_shape=jax.ShapeDtypeStruct((B, T, n_pairs * LANES), BF16),
        scratch_shapes=[pltpu.VMEM((group, BLK, 2 * BLK), F32), pltpu.VMEM((group, BLK, BLK), F32)],
        compiler_params=_cparams("parallel", "parallel"), name="sb_prompt",
    )(q, kb, vb)


def _softmax_step(s, m_ref, l_ref, idx, shift=None):
    n_chunks = s.shape[1] // LANES
    chunk_max = s[:, :LANES]
    for c in range(1, n_chunks):
        chunk_max = jnp.maximum(chunk_max, s[:, c * LANES:(c + 1) * LANES])
    m_old = m_ref[idx]
    row_max = jnp.max(chunk_max, axis=1, keepdims=True)
    m_new = jnp.maximum(m_old, row_max if shift is None else row_max + shift)
    m_ref[idx] = m_new
    p = jnp.exp(s - jnp.concatenate([m_new if shift is None else m_new - shift] * n_chunks, axis=1))
    alpha = jnp.exp(m_old - m_new)
    lane_sum = p[:, :LANES]
    for c in range(1, n_chunks):
        lane_sum = lane_sum + p[:, c * LANES:(c + 1) * LANES]
    l_ref[idx] = alpha * l_ref[idx] + lane_sum
    return p, alpha


def _paired_loop(n, step):
    def pair(k, carry):
        step(2 * k)
        step(2 * k + 1)
        return carry

    lax.fori_loop(0, n // 2, pair, 0)

    @pl.when(n % 2 == 1)
    def _():
        step(n - 1)


def _causal_mask(tb):
    return (lax.broadcasted_iota(jnp.int32, (tb, tb), 1) <= lax.broadcasted_iota(jnp.int32, (tb, tb), 0))


def _fox_prompt_kernel(tb, q_ref, k_ref, v_ref, c_ref, ct_ref, o_ref, kbd_ref, vbd_ref, m_ref, l_ref, acc_ref,
                       cq_ref):
    T = q_ref.shape[1]
    units = acc_ref.shape[0]
    lo16 = _head0_mask(BF16, tb)
    lo32 = _head0_mask(F32, tb)
    cols = [slice(u * LANES, (u + 1) * LANES) for u in range(units)]

    def prepare(j, carry):
        ks = pl.ds(pl.multiple_of(j * tb, tb), tb)
        for u in range(units):
            kbd_ref[u, j] = _block_diag_pair(k_ref[0, ks, cols[u]], lo16)
            vbd_ref[u, j] = _block_diag_pair(v_ref[0, ks, cols[u]], lo16)
        return carry

    lax.fori_loop(0, T // tb, prepare, 0)

    def tile(i, j, masked):
        qs = pl.ds(pl.multiple_of(i * tb, tb), tb)
        ks = pl.ds(pl.multiple_of(j * tb, tb), tb)
        for u in range(units):
            z = _dot_nt(q_ref[0, qs, cols[u]], kbd_ref[u, j])
            ps, alphas = [], []
            for h in range(2):
                s = z[:, h * tb:(h + 1) * tb] - ct_ref[0, u, h:h + 1, ks]
                if masked:
                    s = jnp.where(_causal_mask(tb), s, NEG)
                p, alpha = _softmax_step(s, m_ref, l_ref, 2 * u + h, cq_ref[2 * u + h])
                ps.append(p.astype(BF16))
                alphas.append(alpha)
            pv = _dot(jnp.concatenate(ps, axis=1), vbd_ref[u, j])
            acc_ref[u] = jnp.where(lo32, alphas[0], alphas[1]) * acc_ref[u] + pv

    def qblock(i, carry):
        qs = pl.ds(pl.multiple_of(i * tb, tb), tb)
        m_ref[...] = jnp.full_like(m_ref, NEG)
        l_ref[...] = jnp.zeros_like(l_ref)
        acc_ref[...] = jnp.zeros_like(acc_ref)
        for u in range(units):
            for h in range(2):
                cq_ref[2 * u + h] = jnp.broadcast_to(c_ref[0, u, qs, h:h + 1], (tb, LANES))
        tile(i, i, True)
        _paired_loop(i, lambda j: tile(i, j, False))
        for u in range(units):
            denom = jnp.where(lo32, jnp.sum(l_ref[2 * u], axis=1, keepdims=True),
                              jnp.sum(l_ref[2 * u + 1], axis=1, keepdims=True))
            o_ref[0, qs, cols[u]] = (acc_ref[u] / denom).astype(o_ref.dtype)
        return carry

    lax.fori_loop(0, T // tb, qblock, 0)


def _fox_prompt(q, kb, vb, c, ct, first_pair, n_pairs, *, tb, units):
    B, T, _ = q.shape
    assert n_pairs % units == 0 and first_pair % units == 0
    w = units * LANES
    spec = pl.BlockSpec((1, T, w), lambda b, p: (b, 0, first_pair // units + p))
    c4 = c
    ct4 = ct.reshape(B, n_pairs, 2, T)
    nb = T // tb
    return pl.pallas_call(
        functools.partial(_fox_prompt_kernel, tb), grid=(B, n_pairs // units),
        in_specs=[spec] * 3 + [pl.BlockSpec((1, units, T, 2), lambda b, p: (b, p, 0, 0)),
                               pl.BlockSpec((1, units, 2, T), lambda b, p: (b, p, 0, 0))],
        out_specs=pl.BlockSpec((1, T, w), lambda b, p: (b, 0, p)),
        out_shape=jax.ShapeDtypeStruct((B, T, n_pairs * LANES), BF16),
        scratch_shapes=[pltpu.VMEM((units, nb, 2 * tb, LANES), BF16), pltpu.VMEM((units, nb, 2 * tb, LANES), BF16),
                        pltpu.VMEM((2 * units, tb, LANES), F32), pltpu.VMEM((2 * units, tb, LANES), F32),
                        pltpu.VMEM((units, tb, LANES), F32), pltpu.VMEM((2 * units, tb, LANES), F32)],
        compiler_params=_cparams("parallel", "parallel"), name="fox_prompt",
    )(q, kb, vb, c4, ct4)


def _out_proj_kernel(n_parts, x_ref, *refs):
    o_ref = refs[-1]
    y = x_ref[...]
    for i in range(n_parts):
        y = y + _dot(refs[i][...], refs[n_parts + i][...])
    o_ref[...] = y


def _out_proj(x2, parts, w_out, *, tm):
    M, D = x2.shape
    ws, off = [], 0
    for p in parts:
        ws.append(w_out[off:off + p.shape[1]].astype(BF16))
        off += p.shape[1]
    n = len(parts)
    return pl.pallas_call(
        functools.partial(_out_proj_kernel, n), grid=(M // tm,),
        in_specs=[pl.BlockSpec((tm, D), lambda i: (i, 0))]
                 + [pl.BlockSpec((tm, p.shape[1]), lambda i: (i, 0)) for p in parts]
                 + [pl.BlockSpec(w.shape, lambda i: (0, 0)) for w in ws],
        out_specs=pl.BlockSpec((tm, D), lambda i: (i, 0)),
        out_shape=jax.ShapeDtypeStruct((M, D), F32),
        compiler_params=_cparams("parallel"), name="out_proj",
    )(x2, *parts, *ws)


FF_CHUNK = MXU_DIM
HALO = 2 * SUBLANES


def _silu(x):
    return x * (1.0 / (1.0 + jnp.exp(-x)))


def _ffn_chunk(cs, hcat, hb, g_prev, wg_ref, wu_ref, cw_ref, cb_ref, gs_ref, act_ref):
    tm = hb.shape[0]
    gx = _dot(hcat, wg_ref[:, cs])
    gs_ref[...] = gx
    g0 = gx[gx.shape[0] - tm:]
    g1 = g_prev(1, gs_ref[HALO - 1:HALO - 1 + tm, :])
    g2 = g_prev(2, gs_ref[HALO - 2:HALO - 2 + tm, :])
    gc = cb_ref[:, cs] + cw_ref[0:1, cs] * g2
    gc = gc + cw_ref[1:2, cs] * g1
    gc = gc + cw_ref[2:3, cs] * g0
    act_ref[:, cs] = (_silu(gc) * _dot(hb, wu_ref[:, cs])).astype(BF16)
    return g0


def _ffn_prompt_kernel(n_parts, x_ref, halo_ref, *refs):
    parts, part_halos, w_outs = refs[:n_parts], refs[n_parts:2 * n_parts], refs[2 * n_parts:3 * n_parts]
    g_ref, wg_ref, wu_ref, cw_ref, cb_ref, wd_ref, o_ref, conv_ref, gs_ref, act_ref = refs[3 * n_parts:]
    t = pl.program_id(1)
    tm = x_ref.shape[1]
    xcat = jnp.concatenate([halo_ref[0], x_ref[0]], axis=0)
    for i in range(n_parts):
        xcat = xcat + _dot(jnp.concatenate([part_halos[i][0], parts[i][0]], axis=0), w_outs[i][...])
    x = xcat[HALO:]
    hn = _rmsnorm(xcat, g_ref[...])
    hb = hn[HALO:].astype(BF16)
    hh = jnp.where(t == 0, 0.0, hn[:HALO]).astype(BF16)
    hcat = jnp.concatenate([hh, hb], axis=0)
    last = t == pl.num_programs(1) - 1
    for c in range(wg_ref.shape[1] // FF_CHUNK):
        cs = slice(c * FF_CHUNK, (c + 1) * FF_CHUNK)
        g0 = _ffn_chunk(cs, hcat, hb, lambda k, raw: raw, wg_ref, wu_ref, cw_ref, cb_ref, gs_ref, act_ref)

        @pl.when(last)
        def _():
            conv_ref[0, :, cs] = g0[tm - (CONV_W - 1):, :]
    o_ref[0] = x + _dot(act_ref[...], wd_ref[...])


def _ffn_sample_kernel(seq, x_ref, p1_ref, p2_ref, g_ref, wg_ref, wu_ref, cw_ref, cb_ref, wd_ref, o_ref, gate_ref,
                       gs_ref, act_ref):
    x = x_ref[...]
    tm = x.shape[0]
    hb = _rmsnorm(x, g_ref[...]).astype(BF16)
    hcat = jnp.concatenate([jnp.zeros((HALO, x.shape[1]), BF16), hb], axis=0)
    pos = lax.broadcasted_iota(jnp.int32, (tm, FF_CHUNK), 0) % seq
    for c in range(wg_ref.shape[1] // FF_CHUNK):
        cs = slice(c * FF_CHUNK, (c + 1) * FF_CHUNK)
        prev = lambda k, raw: jnp.where(pos < k, (p1_ref, p2_ref)[k - 1][:, cs], raw)
        gate_ref[:, cs] = _ffn_chunk(cs, hcat, hb, prev, wg_ref, wu_ref, cw_ref, cb_ref, gs_ref, act_ref)
    o_ref[...] = x + _dot(act_ref[...], wd_ref[...])


def _resident(a):
    zeros = (0,) * a.ndim
    return pl.BlockSpec(a.shape, lambda *_: zeros, pipeline_mode=pl.Buffered(1))


def _ffn_weights(g, w_gate, w_up, conv_w, conv_b, w_down):
    return (g.reshape(1, -1), w_gate.astype(BF16), w_up.astype(BF16), conv_w, conv_b.reshape(1, -1),
            w_down.astype(BF16))


def _ffn_prompt(x, parts, w_out, g, w_gate, w_up, conv_w, conv_b, w_down, *, tm):
    B, T, D = x.shape
    dff = w_gate.shape[1]
    ws = _ffn_weights(g, w_gate, w_up, conv_w, conv_b, w_down)
    w_outs, off = [], 0
    for p in parts:
        w_outs.append(w_out[off:off + p.shape[2]].astype(BF16))
        off += p.shape[2]
    hpt = tm // HALO
    main = lambda w: pl.BlockSpec((1, tm, w), lambda b, t: (b, t, 0))
    halo = lambda w: pl.BlockSpec((1, HALO, w), lambda b, t: (b, jnp.maximum(t * hpt - 1, 0), 0))
    return pl.pallas_call(
        functools.partial(_ffn_prompt_kernel, len(parts)), grid=(B, T // tm),
        in_specs=[main(D), halo(D)] + [main(p.shape[2]) for p in parts] + [halo(p.shape[2]) for p in parts]
                 + [_resident(w) for w in w_outs] + [_resident(w) for w in ws],
        out_specs=[pl.BlockSpec((1, tm, D), lambda b, t: (b, t, 0)),
                   pl.BlockSpec((1, CONV_W - 1, dff), lambda b, t: (b, 0, 0))],
        out_shape=[jax.ShapeDtypeStruct((B, T, D), F32), jax.ShapeDtypeStruct((B, CONV_W - 1, dff), F32)],
        scratch_shapes=[pltpu.VMEM((tm + HALO, FF_CHUNK), F32), pltpu.VMEM((tm, dff), BF16)],
        compiler_params=_cparams("parallel", "arbitrary"), name="ffn_prompt",
    )(x, x, *parts, *parts, *w_outs, *ws)


def _ffn_sample(x, state, g, w_gate, w_up, conv_w, conv_b, w_down):
    B, T, D = x.shape
    dff = w_gate.shape[1]
    ws = _ffn_weights(g, w_gate, w_up, conv_w, conv_b, w_down)
    zeros = jnp.zeros((B, T, dff), F32)
    p1 = zeros.at[:, 0].set(state[:, 1]).reshape(B * T, dff)
    p2 = zeros.at[:, 0].set(state[:, 0]).at[:, 1].set(state[:, 1]).reshape(B * T, dff)
    args = (x.reshape(B * T, D), p1, p2) + ws
    y, gate = pl.pallas_call(
        functools.partial(_ffn_sample_kernel, T), grid=(1,),
        in_specs=[_resident(a) for a in args],
        out_specs=[pl.BlockSpec((B * T, D), lambda i: (0, 0)), pl.BlockSpec((B * T, dff), lambda i: (0, 0))],
        out_shape=[jax.ShapeDtypeStruct((B * T, D), F32), jax.ShapeDtypeStruct((B * T, dff), F32)],
        scratch_shapes=[pltpu.VMEM((B * T + HALO, FF_CHUNK), F32), pltpu.VMEM((B * T, dff), BF16)],
        compiler_params=_cparams("arbitrary"), name="ffn_sample",
    )(*args)
    return y.reshape(B, T, D), gate.reshape(B, T, dff)[:, T - (CONV_W - 1):]


def _proj_diff_kernel(x_ref, g_ref, w_ref, qn_ref, kn_ref, gmat_ref, q_ref, k_ref, v_ref, kb_ref, vb_ref):
    wq = q_ref.shape[2]
    hb = _rmsnorm(x_ref[0], g_ref[...]).astype(BF16)
    gmat = gmat_ref[...]
    q_ref[0] = (_head_rmsnorm(_dot(hb, w_ref[:, :wq]), gmat, qn_ref[...]) * QK_SCALE).astype(BF16)
    k = _head_rmsnorm(_dot(hb, w_ref[:, wq:2 * wq]), gmat, kn_ref[...])
    k_ref[0] = k
    kb_ref[0] = k.astype(BF16)
    v = _dot(hb, w_ref[:, 2 * wq:])
    v_ref[0] = v
    vb_ref[0] = v.astype(BF16)


def _proj_diff(x, g, w_in, qn, kn, *, tm):
    B, T, D = x.shape
    wv = w_in.shape[1] // 3
    wq = wv
    reps = wq // HEAD_DIM
    args = (x, g.reshape(1, D), w_in.astype(BF16), jnp.tile(qn, reps).reshape(1, wq),
            jnp.tile(kn, reps).reshape(1, wq), _head_mean_matrix())
    full = lambda a: pl.BlockSpec(a.shape, lambda b, t: (0,) * a.ndim)
    row = lambda w: pl.BlockSpec((1, tm, w), lambda b, t: (b, t, 0))
    sds = lambda w, dt: jax.ShapeDtypeStruct((B, T, w), dt)
    return pl.pallas_call(
        _proj_diff_kernel, grid=(B, T // tm),
        in_specs=[row(D)] + [full(a) for a in args[1:]],
        out_specs=[row(wq), row(wq), row(wv), row(wq), row(wv)],
        out_shape=[sds(wq, BF16), sds(wq, F32), sds(wv, F32), sds(wq, BF16), sds(wv, BF16)],
        compiler_params=_cparams("parallel", "parallel"), name="proj_diff",
    )(*args)


def _t5_bucket(rel):
    rel = np.asarray(rel)
    max_exact = N_BUCKETS // 2
    relf = np.maximum(rel, 1).astype(np.float32)
    large = max_exact + (np.log(relf / np.float32(max_exact)) / np.float32(math.log(MAX_DISTANCE / max_exact))
                         * np.float32(N_BUCKETS - max_exact)).astype(np.int32)
    large = np.minimum(large, N_BUCKETS - 1)
    return np.where(rel < max_exact, rel, large).astype(np.int32)


def _near_buckets(q_pos):
    r = np.asarray(q_pos)[:, None]
    c = np.arange(BLK)[None, :]
    assert _t5_bucket(2 * BLK - (BLK - 1)) == N_BUCKETS - 1
    return jnp.asarray(np.stack([_t5_bucket(np.maximum(r - c, 0)), _t5_bucket(BLK + r - c)]))


def _diff_lambda(lq1_ref, lk1_ref, lq2_ref, lk2_ref, lam_init):
    s1 = jnp.sum(lq1_ref[...] * lk1_ref[...], axis=1, keepdims=True)
    s2 = jnp.sum(lq2_ref[...] * lk2_ref[...], axis=1, keepdims=True)
    return jnp.exp(s1) - jnp.exp(s2) + lam_init


def _diff_prompt_kernel(lam_init, rb_ref, q_ref, k_ref, v_ref, bk_ref, lq1_ref, lk1_ref, lq2_ref, lk2_ref,
                        sub_ref, o_ref, m_ref, l_ref, acc_ref, bias_ref, kbd_ref, vbd_ref):
    T = q_ref.shape[1]
    units, _, tb, _ = bias_ref.shape
    nsub = tb // BLK
    lo16 = _head0_mask(BF16, tb)
    lam = _diff_lambda(lq1_ref, lk1_ref, lq2_ref, lk2_ref, lam_init)
    cols = [slice(u * LANES, (u + 1) * LANES) for u in range(units)]
    heads = [pl.program_id(1) * units + u for u in range(units)]
    far_bias = [rb_ref[N_BUCKETS - 1, h] for h in heads]

    for u in range(units):
        near = []
        for d in range(2):
            tile_bias = jnp.zeros((BLK, BLK), F32)
            for b in range(N_BUCKETS):
                tile_bias = jnp.where(bk_ref[d] == b, rb_ref[b, heads[u]], tile_bias)
            near.append(tile_bias)
        bias_ref[u] = jnp.full(bias_ref.shape[1:], far_bias[u], F32)
        for a in range(nsub):
            bias_ref[u, 0, a * BLK:(a + 1) * BLK, a * BLK:(a + 1) * BLK] = near[0]
            if a >= 1:
                bias_ref[u, 0, a * BLK:(a + 1) * BLK, (a - 1) * BLK:a * BLK] = near[1]
        bias_ref[u, 1, 0:BLK, (nsub - 1) * BLK:nsub * BLK] = near[1]

    def prepare(j, carry):
        ks = pl.ds(pl.multiple_of(j * tb, tb), tb)
        for u in range(units):
            kbd_ref[u, j] = _block_diag_pair(k_ref[0, ks, cols[u]], lo16)
            vb = v_ref[0, ks, cols[u]]
            zero = jnp.zeros_like(vb)
            vbd_ref[u, j] = jnp.concatenate([jnp.concatenate([vb, zero], axis=1),
                                             jnp.concatenate([zero, vb], axis=1)], axis=0)
        return carry

    lax.fori_loop(0, T // tb, prepare, 0)

    def tile(i, j, near, masked):
        qs = pl.ds(pl.multiple_of(i * tb, tb), tb)
        for u in range(units):
            z = _dot_nt(q_ref[0, qs, cols[u]], kbd_ref[u, j])
            ps, alphas = [], []
            for mp in range(2):
                s = z[:, mp * tb:(mp + 1) * tb]
                if near is not None:
                    s = s + bias_ref[u, near]
                if masked:
                    s = jnp.where(_causal_mask(tb), s, NEG)
                p, alpha = _softmax_step(s, m_ref, l_ref, 2 * u + mp, None if near is not None else far_bias[u])
                ps.append(p.astype(BF16))
                alphas.append(alpha)
            pv = _dot(jnp.concatenate(ps, axis=1), vbd_ref[u, j])
            acc_ref[u] = jnp.concatenate(alphas, axis=1) * acc_ref[u] + pv

    def qblock(i, carry):
        qs = pl.ds(pl.multiple_of(i * tb, tb), tb)
        m_ref[...] = jnp.full_like(m_ref, NEG)
        l_ref[...] = jnp.zeros_like(l_ref)
        acc_ref[...] = jnp.zeros_like(acc_ref)
        tile(i, i, 0, True)

        @pl.when(i >= 1)
        def _():
            tile(i, i - 1, 1, False)

        _paired_loop(jnp.maximum(i - 1, 0), lambda j: tile(i, j, None, False))
        for u in range(units):
            o_map = [acc_ref[u, :, mp * LANES:(mp + 1) * LANES] / jnp.sum(l_ref[2 * u + mp], axis=1, keepdims=True)
                     for mp in range(2)]
            o = o_map[0] - lam * o_map[1]
            o_ref[0, qs, cols[u]] = (_rmsnorm(o, sub_ref[...]) * (1.0 - lam_init)).astype(o_ref.dtype)
        return carry

    lax.fori_loop(0, T // tb, qblock, 0)


def _diff_prompt(q, kb, vb, rel_bias, lq1, lk1, lq2, lk2, subln, lam_init, *, tb, units):
    B, T, W = q.shape
    n_heads = W // LANES
    assert n_heads % units == 0
    buckets = _near_buckets(np.arange(BLK))
    spec = pl.BlockSpec((1, T, units * LANES), lambda b, h: (b, 0, h))
    small = lambda a: pl.BlockSpec(a.shape, lambda b, h: (0,) * a.ndim)
    vecs = [a.reshape(1, -1) for a in (lq1, lk1, lq2, lk2, subln)]
    nb = T // tb
    return pl.pallas_call(
        functools.partial(_diff_prompt_kernel, lam_init), grid=(B, n_heads // units),
        in_specs=[pl.BlockSpec(memory_space=pltpu.SMEM)] + [spec] * 3 + [small(buckets)] + [small(a) for a in vecs],
        out_specs=spec,
        scratch_shapes=[pltpu.VMEM((2 * units, tb, LANES), F32), pltpu.VMEM((2 * units, tb, LANES), F32),
                        pltpu.VMEM((units, tb, 2 * LANES), F32), pltpu.VMEM((units, 2, tb, tb), F32),
                        pltpu.VMEM((units, nb, 2 * tb, LANES), BF16),
                        pltpu.VMEM((units, nb, 2 * tb, 2 * LANES), BF16)],
        out_shape=jax.ShapeDtypeStruct((B, T, W), BF16),
        compiler_params=_cparams("parallel", "parallel"), name="diff_prompt",
    )(rel_bias, q, kb, vb, buckets, *vecs)


GROUPS_PER_CHUNK = MXU_DIM // HEAD_DIM


def _chunked_queries(q):
    B, T, W = q.shape
    qg = q.reshape(B, T, W // MXU_DIM, GROUPS_PER_CHUNK, HEAD_DIM).transpose(0, 2, 3, 1, 4)
    eye = jnp.eye(GROUPS_PER_CHUNK, dtype=q.dtype)
    out = qg[:, :, :, :, None, :] * eye[None, None, :, None, :, None]
    return out.reshape(B, W // MXU_DIM, GROUPS_PER_CHUNK * T, MXU_DIM)


def _keys_on_lanes(cache):
    nd = cache.ndim
    t = cache.transpose((0, 1) + tuple(range(3, nd)) + (2,))
    return t.reshape(cache.shape[0], cache.shape[1], -1, cache.shape[2])


def _scores(qc_ref, key_tiles, transposed):
    rows = []
    for c in range(qc_ref.shape[1]):
        cs = slice(c * MXU_DIM, (c + 1) * MXU_DIM)
        if transposed:
            rows.append(_dot(qc_ref[0, c], jnp.concatenate([kt[cs, :].astype(BF16) for kt in key_tiles], axis=1)))
        else:
            rows.append(_dot_nt(qc_ref[0, c], jnp.concatenate([kt[:, cs].astype(BF16) for kt in key_tiles], axis=0)))
    return jnp.concatenate(rows, axis=0)


def _mix_sample_kernel(seq, n_sb, n_pages, pt_ref, qc_ref, kn_ref, vn_ref, lfn_ref, *refs):
    kp = refs[:n_pages]
    vp = refs[n_pages:2 * n_pages]
    lfp = refs[2 * n_pages:3 * n_pages]
    o_ref, run_ref, m_ref, l_ref, acc_ref = refs[3 * n_pages:]
    j = pl.program_id(1)
    sb = n_sb * seq
    n_fox = (BLK - sb) // seq
    rpc = GROUPS_PER_CHUNK * seq
    n_chunks = qc_ref.shape[1]
    suffix = _suffix_matrix()

    def block(z, lfts, pv_of, mask):
        n = len(lfts)
        zs, zf = z[:sb], z[sb:]
        ls = _log_sigmoid_scores(zs)
        l1m = ls - zs
        a_parts, s_parts = [], []
        for g in range(n):
            gs = slice(g * BLK, (g + 1) * BLK)
            elf = jnp.concatenate([jnp.broadcast_to(lfts[g][h:h + 1, :], (seq, BLK)) for h in range(n_fox)], axis=0)
            x = jnp.concatenate([l1m[:, gs], elf], axis=0)
            if mask is not None:
                x = jnp.where(mask, x, 0.0)
            cs = _dot(_split(x), suffix)
            run = run_ref[...]
            run_ref[...] = run + cs[:, BLK:]
            later = cs[:, :BLK] + run
            a = jnp.exp(ls[:, gs] + later[:sb])
            s = zf[:, gs] + later[sb:]
            if mask is not None:
                a = jnp.where(mask[:sb], a, 0.0)
                s = jnp.where(mask[sb:], s, NEG)
            a_parts.append(a)
            s_parts.append(s)
        p, alpha = _softmax_step(jnp.concatenate(s_parts, axis=1), m_ref, l_ref, 0)
        w = jnp.concatenate([jnp.concatenate(a_parts, axis=1), p], axis=0).astype(BF16)
        for c in range(n_chunks):
            pv = pv_of(c, w[c * rpc:(c + 1) * rpc])
            if (c + 1) * rpc <= sb:
                acc_ref[c] += pv
            else:
                al = alpha[c * rpc - sb:(c + 1) * rpc - sb]
                acc_ref[c] = jnp.concatenate([al] * (MXU_DIM // LANES), axis=1) * acc_ref[c] + pv

    @pl.when(j == 0)
    def _():
        run_ref[...] = jnp.zeros_like(run_ref)
        m_ref[...] = jnp.full_like(m_ref, NEG)
        l_ref[...] = jnp.zeros_like(l_ref)
        acc_ref[...] = jnp.zeros_like(acc_ref)
        lane = lax.broadcasted_iota(jnp.int32, (BLK, BLK), 1)
        row = lax.broadcasted_iota(jnp.int32, (BLK, BLK), 0)
        own = lane < row % seq + jnp.where(row < sb, 0, 1)
        kn = _pad_rows(kn_ref[0], BLK)
        vn = _pad_rows(vn_ref[0], BLK).astype(BF16)
        block(_scores(qc_ref, [kn], False), [lfn_ref[0]],
              lambda c, w: _dot(w, vn[:, c * MXU_DIM:(c + 1) * MXU_DIM]), own)

    def page_values(c, w):
        cs = slice(c * MXU_DIM, (c + 1) * MXU_DIM)
        return _dot_nt(w, jnp.concatenate([v[cs, :].astype(BF16) for v in vp], axis=1))

    block(_scores(qc_ref, kp, True), [r[...] for r in lfp], page_values, None)

    @pl.when(j == pl.num_programs(1) - 1)
    def _():
        col_group = lax.broadcasted_iota(jnp.int32, (seq, MXU_DIM), 1) // HEAD_DIM
        denom = jnp.sum(l_ref[0], axis=1, keepdims=True)
        outs = []
        for c in range(n_chunks):
            rows = acc_ref[c]
            if (c + 1) * rpc > sb:
                rows = rows / denom[c * rpc - sb:(c + 1) * rpc - sb]
            out = jnp.zeros((seq, MXU_DIM), F32)
            for g in range(GROUPS_PER_CHUNK):
                out = jnp.where(col_group == g, rows[g * seq:(g + 1) * seq, :], out)
            outs.append(out)
        o_ref[0] = jnp.concatenate(outs, axis=1)


def _paged_specs(shape_tail, layer, n_total, per_step):
    def spec(g):
        zeros = (0,) * len(shape_tail)
        return pl.BlockSpec((None, None) + shape_tail,
                            lambda b, j, pt: (layer, pt[b, n_total - 1 - (j * per_step + g)]) + zeros)
    return [spec(g) for g in range(per_step)]


def _mix_sample(q, k_new, v_new, logf_new, cache_k, cache_v, cache_logf, layer, page_table, n_sb):
    B, T, W = q.shape
    n_heads = W // HEAD_DIM
    n_fox = n_heads - n_sb
    n_total = page_table.shape[1]
    g = min(PAGES_PER_STEP, n_total)
    assert n_heads * T == BLK and cache_k.shape[2] == BLK and n_total % g == 0
    assert (n_sb * T) % (GROUPS_PER_CHUNK * T) == 0
    lfn = jnp.zeros((B, n_fox, BLK), F32).at[:, :, :T].set(logf_new.transpose(0, 2, 1))
    args = (_chunked_queries(q), k_new, v_new, lfn)
    per_seq = lambda a: pl.BlockSpec((1,) + a.shape[1:], lambda b, j, pt: (b,) + (0,) * (a.ndim - 1))
    return pl.pallas_call(
        functools.partial(_mix_sample_kernel, T, n_sb, g),
        grid_spec=pltpu.PrefetchScalarGridSpec(
            num_scalar_prefetch=1, grid=(B, n_total // g),
            in_specs=[per_seq(a) for a in args] + _paged_specs((W, BLK), layer, n_total, g) * 2
                     + _paged_specs((n_fox, BLK), layer, n_total, g),
            out_specs=pl.BlockSpec((1, T, W), lambda b, j, pt: (b, 0, 0)),
            scratch_shapes=[pltpu.VMEM((BLK, BLK), F32), pltpu.VMEM((1, n_fox * T, BLK), F32),
                            pltpu.VMEM((1, n_fox * T, BLK), F32),
                            pltpu.VMEM((W // MXU_DIM, GROUPS_PER_CHUNK * T, MXU_DIM), F32)]),
        out_shape=jax.ShapeDtypeStruct((B, T, W), F32),
        compiler_params=_cparams("parallel", "arbitrary"), name="mix_sample",
    )(page_table, *args, *([_keys_on_lanes(cache_k)] * g), *([_keys_on_lanes(cache_v)] * g),
      *([_keys_on_lanes(cache_logf)] * g))


def _diff_sample_kernel(seq, lam_init, n_pages, pt_ref, qc_ref, kn_ref, vn_ref, bk_ref, rbr_ref,
                        lq1_ref, lk1_ref, lq2_ref, lk2_ref, sub_ref, *refs):
    kp = refs[:n_pages]
    vp = refs[n_pages:2 * n_pages]
    o_ref, m_ref, l_ref, acc_ref, bias_ref = refs[2 * n_pages:]
    j = pl.program_id(1)
    n_heads = vn_ref.shape[2] // BLK
    rph = 2 * seq

    def near_bias(d):
        out = jnp.zeros((BLK, BLK), F32)
        for b in range(N_BUCKETS):
            out = jnp.where(bk_ref[d] == b, rbr_ref[:, b:b + 1], out)
        return out

    def block(s, values_of, mask):
        if mask is not None:
            s = jnp.where(mask, s, NEG)
        p, alpha = _softmax_step(s, m_ref, l_ref, 0)
        w = p.astype(BF16)
        pv = jnp.concatenate([_dot(w[h * rph:(h + 1) * rph], values_of(h)) for h in range(n_heads)], axis=0)
        acc_ref[...] = alpha * acc_ref[...] + pv

    far = jnp.broadcast_to(rbr_ref[:, N_BUCKETS - 1:N_BUCKETS], (BLK, BLK))

    @pl.when(j == 0)
    def _():
        m_ref[...] = jnp.full_like(m_ref, NEG)
        l_ref[...] = jnp.zeros_like(l_ref)
        acc_ref[...] = jnp.zeros_like(acc_ref)
        lane = lax.broadcasted_iota(jnp.int32, (BLK, BLK), 1)
        row = lax.broadcasted_iota(jnp.int32, (BLK, BLK), 0)
        kn = _pad_rows(kn_ref[0], BLK)
        block(_scores(qc_ref, [kn], False) + near_bias(0),
              lambda h: _pad_rows(vn_ref[0, :, h * BLK:(h + 1) * BLK], BLK).astype(BF16), lane <= row % seq)
        bias_ref[...] = jnp.concatenate([near_bias(1)] + [far] * (n_pages - 1), axis=1)

    @pl.when(j == 1)
    def _():
        bias_ref[...] = jnp.concatenate([far] * n_pages, axis=1)

    block(_scores(qc_ref, kp, True) + bias_ref[...],
          lambda h: jnp.concatenate([v[pl.ds(h, BLK, stride=n_heads), :].astype(BF16) for v in vp], axis=0), None)

    @pl.when(j == pl.num_programs(1) - 1)
    def _():
        rows = acc_ref[...] / jnp.sum(l_ref[0], axis=1, keepdims=True)
        lam = _diff_lambda(lq1_ref, lk1_ref, lq2_ref, lk2_ref, lam_init)
        sub = sub_ref[...]
        outs = []
        for h in range(n_heads):
            o = rows[h * rph:h * rph + seq] - lam * rows[h * rph + seq:(h + 1) * rph]
            outs.append(_rmsnorm(o, sub))
        o_ref[0] = jnp.concatenate(outs, axis=1) * (1.0 - lam_init)


def _diff_sample(q, k_new, v_new, cache_k, cache_v, layer, page_table, rel_bias, lq1, lk1, lq2, lk2, subln, lam_init):
    B, T, W = q.shape
    n_heads = cache_v.shape[3]
    n_total = page_table.shape[1]
    g = min(PAGES_PER_STEP, n_total)
    assert (W // HEAD_DIM) * T == BLK and cache_k.shape[2] == BLK and n_total % g == 0
    assert cache_v.shape[4] == BLK and v_new.shape[2] == n_heads * BLK
    L, P = cache_v.shape[:2]
    v_rows = cache_v.reshape(L, P, BLK * n_heads, BLK)
    buckets = _near_buckets(np.arange(BLK) % T)
    rb_rows = jnp.repeat(rel_bias.T, 2 * T, axis=0)
    vecs = [a.reshape(1, -1) for a in (lq1, lk1, lq2, lk2, subln)]
    args = (_chunked_queries(q), k_new, v_new)
    per_seq = lambda a: pl.BlockSpec((1,) + a.shape[1:], lambda b, j, pt: (b,) + (0,) * (a.ndim - 1))
    small = lambda a: pl.BlockSpec(a.shape, lambda b, j, pt: (0,) * a.ndim)
    return pl.pallas_call(
        functools.partial(_diff_sample_kernel, T, lam_init, g),
        grid_spec=pltpu.PrefetchScalarGridSpec(
            num_scalar_prefetch=1, grid=(B, n_total // g),
            in_specs=[per_seq(a) for a in args] + [small(buckets), small(rb_rows)] + [small(a) for a in vecs]
                     + _paged_specs((W, BLK), layer, n_total, g) + _paged_specs((BLK * n_heads, BLK), layer, n_total, g),
            out_specs=pl.BlockSpec((1, T, n_heads * BLK), lambda b, j, pt: (b, 0, 0)),
            scratch_shapes=[pltpu.VMEM((1, BLK, BLK), F32), pltpu.VMEM((1, BLK, BLK), F32),
                            pltpu.VMEM((BLK, BLK), F32), pltpu.VMEM((BLK, g * BLK), F32)]),
        out_shape=jax.ShapeDtypeStruct((B, T, n_heads * BLK), F32),
        compiler_params=_cparams("parallel", "arbitrary"), name="diff_sample",
    )(page_table, *args, buckets, rb_rows, *vecs, *([_keys_on_lanes(cache_k)] * g), *([v_rows] * g))


def kernel(x_prompt, x_sample, cache_k_mix, cache_v_mix, cache_logf_mix, cache_k_diff, cache_v_diff, state_ffn_conv, page_table, rel_bias, norm_mix, w_in_mix, b_forget, qnorm_fox, knorm_fox, w_out_mix, norm_diff, w_in_diff, qnorm_diff, knorm_diff, lambda_q1, lambda_k1, lambda_q2, lambda_k2, subln_diff, w_out_diff, norm_ffn, w_gate, w_up, conv_w, conv_b, w_down):
    B, T, D = x_prompt.shape
    DB, DT, _ = x_sample.shape
    depth = norm_ffn.shape[0]
    n_fox = b_forget.shape[1]
    n_mix = cache_k_mix.shape[3]
    n_sb = n_mix - n_fox
    n_diff = cache_k_diff.shape[3]
    sb_pairs = n_sb * HEAD_DIM // LANES
    fox_pairs = n_fox * HEAD_DIM // LANES
    tm = min(T, PROMPT_ROW_TILE)
    tb = min(T, ATT_BLOCK)
    xp, xs = x_prompt, x_sample
    outs = {n: [] for n in ("kmp", "vmp", "lfp", "kms", "vms", "lfs", "kdp", "vdp", "kds", "vds", "cp", "cs")}
    for l in range(depth):
        j = l // 2
        if l % 2 == 0:
            pw = (norm_mix[j], w_in_mix[j], b_forget[j], qnorm_fox[j], knorm_fox[j])
            q, k, v, kb, vb, logf, c, ct = _proj_mix(xp, *pw, tm=tm, with_cumsum=True)
            o_sb = _sb_prompt(q, kb, vb, sb_pairs, group=min(T // BLK, SB_GROUP))
            o_fx = _fox_prompt(q, kb, vb, c, ct, sb_pairs, fox_pairs, tb=tb, units=ATT_UNITS)
            mixed, w_out = [o_sb, o_fx], w_out_mix[j]
            outs["kmp"].append(k.reshape(B, T, n_mix, HEAD_DIM))
            outs["vmp"].append(v.reshape(B, T, n_mix, HEAD_DIM))
            outs["lfp"].append(logf)
            qs, ks, vs, _, _, lfs = _proj_mix(xs.reshape(1, DB * DT, D), *pw, tm=DB * DT, with_cumsum=False)
            ks, vs, lfs = ks.reshape(DB, DT, -1), vs.reshape(DB, DT, -1), lfs.reshape(DB, DT, n_fox)
            o = _mix_sample(qs.reshape(DB, DT, -1), ks, vs, lfs, cache_k_mix, cache_v_mix, cache_logf_mix, j,
                            page_table, n_sb)
            xs = _out_proj(xs.reshape(DB * DT, D), [o.reshape(DB * DT, -1).astype(BF16)], w_out_mix[j],
                           tm=DB * DT).reshape(DB, DT, D)
            outs["kms"].append(ks.reshape(DB, DT, n_mix, HEAD_DIM))
            outs["vms"].append(vs.reshape(DB, DT, n_mix, HEAD_DIM))
            outs["lfs"].append(lfs)
        else:
            lam_init = 0.8 - 0.6 * math.exp(-0.3 * l)
            pw = (norm_diff[j], w_in_diff[j], qnorm_diff[j], knorm_diff[j])
            lam_w = (lambda_q1[j], lambda_k1[j], lambda_q2[j], lambda_k2[j], subln_diff[j], lam_init)
            q, k, v, kb, vb = _proj_diff(xp, *pw, tm=tm)
            o = _diff_prompt(q, kb, vb, rel_bias, *lam_w, tb=tb, units=ATT_UNITS)
            mixed, w_out = [o], w_out_diff[j]
            outs["kdp"].append(k.reshape(B, T, n_diff, 2, HEAD_DIM))
            outs["vdp"].append(v.reshape(B, T, n_diff, 2 * HEAD_DIM))
            qs, ks, vs, _, _ = _proj_diff(xs.reshape(1, DB * DT, D), *pw, tm=DB * DT)
            ks, vs = ks.reshape(DB, DT, -1), vs.reshape(DB, DT, -1)
            o = _diff_sample(qs.reshape(DB, DT, -1), ks, vs, cache_k_diff, cache_v_diff, j, page_table, rel_bias,
                             *lam_w)
            xs = _out_proj(xs.reshape(DB * DT, D), [o.reshape(DB * DT, -1).astype(BF16)], w_out_diff[j],
                           tm=DB * DT).reshape(DB, DT, D)
            outs["kds"].append(ks.reshape(DB, DT, n_diff, 2, HEAD_DIM))
            outs["vds"].append(vs.reshape(DB, DT, n_diff, 2 * HEAD_DIM))
        fw = (norm_ffn[l], w_gate[l], w_up[l], conv_w[l], conv_b[l], w_down[l])
        xp, cp = _ffn_prompt(xp, mixed, w_out, *fw, tm=min(T, FFN_ROW_TILE))
        xs, cs = _ffn_sample(xs, state_ffn_conv[l], *fw)
        outs["cp"].append(cp)
        outs["cs"].append(cs)
    st = {n: jnp.stack(a) for n, a in outs.items()}
    return (xp, xs, st["kmp"], st["vmp"], st["lfp"], st["kms"], st["vms"], st["lfs"], st["kdp"], st["vdp"],
            st["kds"], st["vds"], st["cp"], st["cs"])
```

```python
import functools
import math

import numpy as np
import jax
import jax.numpy as jnp
from jax import lax
from jax.experimental import pallas as pl
from jax.experimental.pallas import tpu as pltpu

F32 = jnp.float32
BF16 = jnp.bfloat16

HEAD_DIM = 64
N_BUCKETS = 32
MAX_DISTANCE = 128
CONV_W = 3
EPS = 1e-6
QK_SCALE = HEAD_DIM ** -0.5

LANES = 128
SUBLANES = 8
MXU_DIM = 256
VMEM_LIMIT = 56 * 1024 * 1024

BLK = LANES
NEG = -1e30
EXP_ZERO_BELOW = -104.0

PROMPT_ROW_TILE = 512
FFN_ROW_TILE = 1024
ATT_BLOCK = 512
ATT_UNITS = 2
SB_GROUP = 8
PAGES_PER_STEP = 16


def _cparams(*sem):
    return pltpu.CompilerParams(dimension_semantics=sem, vmem_limit_bytes=VMEM_LIMIT)


def _dot(a, b):
    return jnp.dot(a, b, preferred_element_type=F32)


def _dot_nt(a, b):
    return lax.dot_general(a, b, (((1,), (1,)), ((), ())), preferred_element_type=F32)


def _dot_f32(a, b):
    return jnp.dot(a, b, preferred_element_type=F32, precision=lax.Precision.HIGHEST)


def _split(a):
    hi = a.astype(BF16)
    lo = (a - hi.astype(F32)).astype(BF16)
    return jnp.concatenate([hi, lo], axis=1)


def _log_sigmoid(x):
    return jnp.minimum(x, 0.0) - jnp.log1p(jnp.exp(-jnp.abs(x)))


def _log_sigmoid_scores(x):
    return jnp.minimum(x, 0.0) - jnp.log(1.0 + jnp.exp(-jnp.abs(x)))


def _rmsnorm(x, g):
    return x * lax.rsqrt(jnp.mean(x * x, axis=-1, keepdims=True) + EPS) * g


def _head_rmsnorm(y, gmat2, gain):
    outs = []
    for c in range(y.shape[1] // LANES):
        yc = y[:, c * LANES:(c + 1) * LANES]
        outs.append(yc * lax.rsqrt(_dot(_split(yc * yc), gmat2) + EPS))
    return jnp.concatenate(outs, axis=1) * gain


def _head_mean_matrix():
    idx = np.arange(LANES) // HEAD_DIM
    g = (idx[:, None] == idx[None, :]).astype(np.float32) / HEAD_DIM
    return jnp.asarray(np.concatenate([g, g], axis=0), BF16)


def _suffix_matrix():
    r = lax.broadcasted_iota(jnp.int32, (2 * BLK, 2 * BLK), 0) & (BLK - 1)
    c = lax.broadcasted_iota(jnp.int32, (2 * BLK, 2 * BLK), 1)
    return jnp.where((c >= BLK) | (r > c), 1.0, 0.0).astype(BF16)


def _head0_mask(dtype, rows=BLK):
    lane = lax.broadcasted_iota(jnp.int32, (rows, LANES), 1)
    if dtype == F32:
        return lane < HEAD_DIM
    return jnp.where(lane < HEAD_DIM, 1.0, 0.0).astype(dtype) > 0


def _block_diag_pair(x, lo_half):
    zero = jnp.zeros_like(x)
    return jnp.concatenate([jnp.where(lo_half, x, zero), jnp.where(lo_half, zero, x)], axis=0)


def _pad_rows(x, n):
    return jnp.concatenate([x, jnp.zeros((n - x.shape[0],) + x.shape[1:], x.dtype)], axis=0)


def _proj_mix_kernel(with_cumsum, x_ref, g_ref, w_ref, wf_ref, wft_ref, bfr_ref, bfc_ref, qn_ref, kn_ref,
                     gmat_ref, q_ref, k_ref, v_ref, kb_ref, vb_ref, logf_ref, *rest):
    ws = q_ref.shape[2] // 2
    n_fox = logf_ref.shape[2]
    tm = x_ref.shape[1]
    hb = _rmsnorm(x_ref[0], g_ref[...]).astype(BF16)

    def sec(i):
        return _dot(hb, w_ref[:, i * ws:(i + 1) * ws])

    q_ref[0, :, :ws] = (sec(0) * QK_SCALE).astype(BF16)
    k_sb = sec(1)
    k_ref[0, :, :ws] = k_sb
    kb_ref[0, :, :ws] = k_sb.astype(BF16)
    v_sb = sec(2)
    v_ref[0, :, :ws] = v_sb
    vb_ref[0, :, :ws] = v_sb.astype(BF16)
    gmat = gmat_ref[...]
    q_ref[0, :, ws:] = (_head_rmsnorm(sec(3), gmat, qn_ref[...]) * QK_SCALE).astype(BF16)
    k_fx = _head_rmsnorm(sec(4), gmat, kn_ref[...])
    k_ref[0, :, ws:] = k_fx
    kb_ref[0, :, ws:] = k_fx.astype(BF16)
    v_fx = sec(5)
    v_ref[0, :, ws:] = v_fx
    vb_ref[0, :, ws:] = v_fx.astype(BF16)

    logf = _log_sigmoid(_dot(hb, wf_ref[...]) + bfr_ref[...])
    logf_ref[0] = logf[:, :n_fox]
    if not with_cumsum:
        return
    c_ref, ct_ref, carry_ref, carryt_ref = rest

    @pl.when(pl.program_id(1) == 0)
    def _():
        carry_ref[...] = jnp.zeros_like(carry_ref)
        carryt_ref[...] = jnp.zeros_like(carryt_ref)

    r = lax.broadcasted_iota(jnp.int32, (tm, tm), 0)
    c = lax.broadcasted_iota(jnp.int32, (tm, tm), 1)
    csum = _dot_f32(jnp.where(c <= r, 1.0, 0.0).astype(F32), logf) + carry_ref[...]
    for p in range(n_fox // 2):
        c_ref[0, p] = csum[:, 2 * p:2 * p + 2]
    carry_ref[...] = csum[tm - 1:tm, :]
    logft = _log_sigmoid(_dot_nt(wft_ref[...], hb) + bfc_ref[:, 0:1])
    csumt = _dot_f32(logft, jnp.where(r <= c, 1.0, 0.0).astype(F32)) + carryt_ref[:, 0:1]
    ct_ref[0] = csumt[:n_fox, :]
    carryt_ref[...] = jnp.broadcast_to(csumt[:, tm - 1:tm], carryt_ref.shape)


def _proj_mix(x, g, w_in, b_f, qn, kn, *, tm, with_cumsum):
    B, T, D = x.shape
    n_fox = b_f.shape[0]
    ws = (w_in.shape[1] - n_fox) // 6
    wmain = w_in[:, :6 * ws].astype(BF16)
    wf = jnp.zeros((D, LANES), BF16).at[:, :n_fox].set(w_in[:, 6 * ws:].astype(BF16))
    wft = jnp.zeros((2 * SUBLANES, D), BF16).at[:n_fox, :].set(w_in[:, 6 * ws:].T.astype(BF16))
    bfr = jnp.zeros((1, LANES), F32).at[0, :n_fox].set(b_f)
    bfc = jnp.zeros((2 * SUBLANES, LANES), F32).at[:n_fox, :].set(jnp.broadcast_to(b_f[:, None], (n_fox, LANES)))
    reps = ws // HEAD_DIM
    full = lambda a: pl.BlockSpec(a.shape, lambda b, t: (0,) * a.ndim)
    row = lambda w: pl.BlockSpec((1, tm, w), lambda b, t: (b, t, 0))
    args = (x, g.reshape(1, D), wmain, wf, wft, bfr, bfc, jnp.tile(qn, reps).reshape(1, ws),
            jnp.tile(kn, reps).reshape(1, ws), _head_mean_matrix())
    out_shape = [jax.ShapeDtypeStruct((B, T, 2 * ws), BF16), jax.ShapeDtypeStruct((B, T, 2 * ws), F32),
                 jax.ShapeDtypeStruct((B, T, 2 * ws), F32), jax.ShapeDtypeStruct((B, T, 2 * ws), BF16),
                 jax.ShapeDtypeStruct((B, T, 2 * ws), BF16), jax.ShapeDtypeStruct((B, T, n_fox), F32)]
    out_specs = [row(2 * ws)] * 5 + [row(n_fox)]
    scratch = []
    if with_cumsum:
        out_shape += [jax.ShapeDtypeStruct((B, n_fox // 2, T, 2), F32), jax.ShapeDtypeStruct((B, n_fox, T), F32)]
        out_specs += [pl.BlockSpec((1, n_fox // 2, tm, 2), lambda b, t: (b, 0, t, 0)),
                      pl.BlockSpec((1, n_fox, tm), lambda b, t: (b, 0, t))]
        scratch = [pltpu.VMEM((1, LANES), F32), pltpu.VMEM((2 * SUBLANES, LANES), F32)]
    return pl.pallas_call(
        functools.partial(_proj_mix_kernel, with_cumsum),
        grid=(B, T // tm),
        in_specs=[row(D)] + [full(a) for a in args[1:]],
        out_specs=out_specs, out_shape=out_shape, scratch_shapes=scratch,
        compiler_params=_cparams("parallel", "arbitrary"), name="proj_mix",
    )(*args)


def _sb_prompt_kernel(q_ref, k_ref, v_ref, o_ref, r_ref, acc_ref):
    T = q_ref.shape[1]
    group = r_ref.shape[0]
    lane = lax.broadcasted_iota(jnp.int32, (BLK, BLK), 1)
    row = lax.broadcasted_iota(jnp.int32, (BLK, BLK), 0)
    lo_half = _head0_mask(BF16)
    strict = lane < row
    strict2 = jnp.concatenate([strict, strict], axis=1)
    suffix = _suffix_matrix()

    def tile(a, iq, ik, valid, masked):
        q2 = q_ref[0, pl.ds(pl.multiple_of(iq * BLK, BLK), BLK), :]
        ks = pl.ds(pl.multiple_of(ik * BLK, BLK), BLK)
        z = _dot_nt(q2, _block_diag_pair(k_ref[0, ks, :], lo_half))
        ls = _log_sigmoid_scores(z)
        l1m = ls - z
        if masked:
            l1m = jnp.where(strict2, l1m, 0.0)
        w = []
        for h in range(2):
            hs = slice(h * BLK, (h + 1) * BLK)
            cs = _dot(_split(l1m[:, hs]), suffix)
            run = r_ref[a, :, hs]
            wh = jnp.exp(ls[:, hs] + cs[:, :BLK] + run)
            total = cs[:, BLK:]
            if masked:
                wh = jnp.where(strict, wh, 0.0)
            if valid is not None:
                wh = wh * valid
                total = total * valid
            w.append(wh.astype(BF16))
            r_ref[a, :, hs] = run + total
        acc_ref[a] += _dot(jnp.concatenate(w, axis=1), _block_diag_pair(v_ref[0, ks, :], lo_half))

    def qgroup(i, carry):
        base = i * group
        r_ref[...] = jnp.zeros_like(r_ref)
        acc_ref[...] = jnp.zeros_like(acc_ref)
        for a in range(group):
            tile(a, base + a, base + a, None, True)

        def alive(d):
            worst = jnp.full((BLK, 2 * BLK), NEG, F32)
            for a in range(group):
                worst = jnp.maximum(worst, jnp.where(base + a - d >= 0, r_ref[a], NEG))
            return (jnp.max(worst) > EXP_ZERO_BELOW).astype(jnp.int32)

        def cond(s):
            return s[1] > 0

        def body(s):
            d = s[0]
            for a in range(group):
                ik = base + a - d
                tile(a, base + a, jnp.maximum(ik, 0), (ik >= 0).astype(F32), False)
            return d + 1, alive(d + 1)

        lax.while_loop(cond, body, (1, alive(1)))
        for a in range(group):
            o_ref[0, pl.ds(pl.multiple_of((base + a) * BLK, BLK), BLK), :] = acc_ref[a].astype(o_ref.dtype)
        return carry

    lax.fori_loop(0, T // (group * BLK), qgroup, 0)


def _sb_prompt(q, kb, vb, n_pairs, *, group):
    B, T, _ = q.shape
    assert T % (group * BLK) == 0
    spec = pl.BlockSpec((1, T, LANES), lambda b, p: (b, 0, p))
    return pl.pallas_call(
        _sb_prompt_kernel, grid=(B, n_pairs), in_specs=[spec] * 3, out_specs=spec,
        out_shape=jax.ShapeDtypeStruct((B, T, n_pairs * LANES), BF16),
        scratch_shapes=[pltpu.VMEM((group, BLK, 2 * BLK), F32), pltpu.VMEM((group, BLK, BLK), F32)],
        compiler_params=_cparams("parallel", "parallel"), name="sb_prompt",
    )(q, kb, vb)


def _softmax_step(s, m_ref, l_ref, idx, shift=None):
    n_chunks = s.shape[1] // LANES
    chunk_max = s[:, :LANES]
    for c in range(1, n_chunks):
        chunk_max = jnp.maximum(chunk_max, s[:, c * LANES:(c + 1) * LANES])
    m_old = m_ref[idx]
    row_max = jnp.max(chunk_max, axis=1, keepdims=True)
    m_new = jnp.maximum(m_old, row_max if shift is None else row_max + shift)
    m_ref[idx] = m_new
    p = jnp.exp(s - jnp.concatenate([m_new if shift is None else m_new - shift] * n_chunks, axis=1))
    alpha = jnp.exp(m_old - m_new)
    lane_sum = p[:, :LANES]
    for c in range(1, n_chunks):
        lane_sum = lane_sum + p[:, c * LANES:(c + 1) * LANES]
    l_ref[idx] = alpha * l_ref[idx] + lane_sum
    return p, alpha


def _paired_loop(n, step):
    def pair(k, carry):
        step(2 * k)
        step(2 * k + 1)
        return carry

    lax.fori_loop(0, n // 2, pair, 0)

    @pl.when(n % 2 == 1)
    def _():
        step(n - 1)


def _causal_mask(tb):
    return (lax.broadcasted_iota(jnp.int32, (tb, tb), 1) <= lax.broadcasted_iota(jnp.int32, (tb, tb), 0))


def _fox_prompt_kernel(tb, q_ref, k_ref, v_ref, c_ref, ct_ref, o_ref, kbd_ref, vbd_ref, m_ref, l_ref, acc_ref,
                       cq_ref):
    T = q_ref.shape[1]
    units = acc_ref.shape[0]
    lo16 = _head0_mask(BF16, tb)
    lo32 = _head0_mask(F32, tb)
    cols = [slice(u * LANES, (u + 1) * LANES) for u in range(units)]

    def prepare(j, carry):
        ks = pl.ds(pl.multiple_of(j * tb, tb), tb)
        for u in range(units):
            kbd_ref[u, j] = _block_diag_pair(k_ref[0, ks, cols[u]], lo16)
            vbd_ref[u, j] = _block_diag_pair(v_ref[0, ks, cols[u]], lo16)
        return carry

    lax.fori_loop(0, T // tb, prepare, 0)

    def tile(i, j, masked):
        qs = pl.ds(pl.multiple_of(i * tb, tb), tb)
        ks = pl.ds(pl.multiple_of(j * tb, tb), tb)
        for u in range(units):
            z = _dot_nt(q_ref[0, qs, cols[u]], kbd_ref[u, j])
            ps, alphas = [], []
            for h in range(2):
                s = z[:, h * tb:(h + 1) * tb] - ct_ref[0, u, h:h + 1, ks]
                if masked:
                    s = jnp.where(_causal_mask(tb), s, NEG)
                p, alpha = _softmax_step(s, m_ref, l_ref, 2 * u + h, cq_ref[2 * u + h])
                ps.append(p.astype(BF16))
                alphas.append(alpha)
            pv = _dot(jnp.concatenate(ps, axis=1), vbd_ref[u, j])
            acc_ref[u] = jnp.where(lo32, alphas[0], alphas[1]) * acc_ref[u] + pv

    def qblock(i, carry):
        qs = pl.ds(pl.multiple_of(i * tb, tb), tb)
        m_ref[...] = jnp.full_like(m_ref, NEG)
        l_ref[...] = jnp.zeros_like(l_ref)
        acc_ref[...] = jnp.zeros_like(acc_ref)
        for u in range(units):
            for h in range(2):
                cq_ref[2 * u + h] = jnp.broadcast_to(c_ref[0, u, qs, h:h + 1], (tb, LANES))
        tile(i, i, True)
        _paired_loop(i, lambda j: tile(i, j, False))
        for u in range(units):
            denom = jnp.where(lo32, jnp.sum(l_ref[2 * u], axis=1, keepdims=True),
                              jnp.sum(l_ref[2 * u + 1], axis=1, keepdims=True))
            o_ref[0, qs, cols[u]] = (acc_ref[u] / denom).astype(o_ref.dtype)
        return carry

    lax.fori_loop(0, T // tb, qblock, 0)


def _fox_prompt(q, kb, vb, c, ct, first_pair, n_pairs, *, tb, units):
    B, T, _ = q.shape
    assert n_pairs % units == 0 and first_pair % units == 0
    w = units * LANES
    spec = pl.BlockSpec((1, T, w), lambda b, p: (b, 0, first_pair // units + p))
    c4 = c
    ct4 = ct.reshape(B, n_pairs, 2, T)
    nb = T // tb
    return pl.pallas_call(
        functools.partial(_fox_prompt_kernel, tb), grid=(B, n_pairs // units),
        in_specs=[spec] * 3 + [pl.BlockSpec((1, units, T, 2), lambda b, p: (b, p, 0, 0)),
                               pl.BlockSpec((1, units, 2, T), lambda b, p: (b, p, 0, 0))],
        out_specs=pl.BlockSpec((1, T, w), lambda b, p: (b, 0, p)),
        out_shape=jax.ShapeDtypeStruct((B, T, n_pairs * LANES), BF16),
        scratch_shapes=[pltpu.VMEM((units, nb, 2 * tb, LANES), BF16), pltpu.VMEM((units, nb, 2 * tb, LANES), BF16),
                        pltpu.VMEM((2 * units, tb, LANES), F32), pltpu.VMEM((2 * units, tb, LANES), F32),
                        pltpu.VMEM((units, tb, LANES), F32), pltpu.VMEM((2 * units, tb, LANES), F32)],
        compiler_params=_cparams("parallel", "parallel"), name="fox_prompt",
    )(q, kb, vb, c4, ct4)


def _out_proj_kernel(n_parts, x_ref, *refs):
    o_ref = refs[-1]
    y = x_ref[...]
    for i in range(n_parts):
        y = y + _dot(refs[i][...], refs[n_parts + i][...])
    o_ref[...] = y


def _out_proj(x2, parts, w_out, *, tm):
    M, D = x2.shape
    ws, off = [], 0
    for p in parts:
        ws.append(w_out[off:off + p.shape[1]].astype(BF16))
        off += p.shape[1]
    n = len(parts)
    return pl.pallas_call(
        functools.partial(_out_proj_kernel, n), grid=(M // tm,),
        in_specs=[pl.BlockSpec((tm, D), lambda i: (i, 0))]
                 + [pl.BlockSpec((tm, p.shape[1]), lambda i: (i, 0)) for p in parts]
                 + [pl.BlockSpec(w.shape, lambda i: (0, 0)) for w in ws],
        out_specs=pl.BlockSpec((tm, D), lambda i: (i, 0)),
        out_shape=jax.ShapeDtypeStruct((M, D), F32),
        compiler_params=_cparams("parallel"), name="out_proj",
    )(x2, *parts, *ws)


FF_CHUNK = MXU_DIM
HALO = 2 * SUBLANES


def _silu(x):
    return x * (1.0 / (1.0 + jnp.exp(-x)))


def _ffn_chunk(cs, h, g_prev, wg_ref, wu_ref, cw_ref, cb_ref, gs_ref, act_ref):
    tm = h.shape[0] - HALO
    gx = _dot(h[...], wg_ref[:, cs])
    gs_ref[...] = gx
    g0 = gx[gx.shape[0] - tm:]
    g1 = g_prev(1, gs_ref[HALO - 1:HALO - 1 + tm, :])
    g2 = g_prev(2, gs_ref[HALO - 2:HALO - 2 + tm, :])
    gc = cb_ref[:, cs] + cw_ref[0:1, cs] * g2
    gc = gc + cw_ref[1:2, cs] * g1
    gc = gc + cw_ref[2:3, cs] * g0
    act_ref[:, cs] = (_silu(gc) * _dot(h[HALO:], wu_ref[:, cs])).astype(BF16)
    return g0


def _ffn_prompt_kernel(n_parts, x_ref, halo_ref, *refs):
    parts, part_halos, w_outs = refs[:n_parts], refs[n_parts:2 * n_parts], refs[2 * n_parts:3 * n_parts]
    g_ref, wg_ref, wu_ref, cw_ref, cb_ref, wd_ref, o_ref, conv_ref, gs_ref, act_ref, h_ref = refs[3 * n_parts:]
    t = pl.program_id(1)
    tm = x_ref.shape[1]
    xcat = jnp.concatenate([halo_ref[0], x_ref[0]], axis=0)
    for i in range(n_parts):
        xcat = xcat + _dot(jnp.concatenate([part_halos[i][0], parts[i][0]], axis=0), w_outs[i][...])
    o_ref[0] = xcat[HALO:]
    hn = _rmsnorm(xcat, g_ref[...])
    h_ref[:HALO, :] = jnp.where(t == 0, 0.0, hn[:HALO]).astype(BF16)
    h_ref[HALO:, :] = hn[HALO:].astype(BF16)
    last = t == pl.num_programs(1) - 1
    for c in range(wg_ref.shape[1] // FF_CHUNK):
        cs = slice(c * FF_CHUNK, (c + 1) * FF_CHUNK)
        g0 = _ffn_chunk(cs, h_ref, lambda k, raw: raw, wg_ref, wu_ref, cw_ref, cb_ref, gs_ref, act_ref)

        @pl.when(last)
        def _():
            conv_ref[0, :, cs] = g0[tm - (CONV_W - 1):, :]
    o_ref[0] += _dot(act_ref[...], wd_ref[...])


def _ffn_sample_kernel(seq, x_ref, p1_ref, p2_ref, g_ref, wg_ref, wu_ref, cw_ref, cb_ref, wd_ref, o_ref, gate_ref,
                       gs_ref, act_ref):
    x = x_ref[...]
    tm = x.shape[0]
    hb = _rmsnorm(x, g_ref[...]).astype(BF16)
    hcat = jnp.concatenate([jnp.zeros((HALO, x.shape[1]), BF16), hb], axis=0)
    pos = lax.broadcasted_iota(jnp.int32, (tm, FF_CHUNK), 0) % seq
    for c in range(wg_ref.shape[1] // FF_CHUNK):
        cs = slice(c * FF_CHUNK, (c + 1) * FF_CHUNK)
        prev = lambda k, raw: jnp.where(pos < k, (p1_ref, p2_ref)[k - 1][:, cs], raw)
        gate_ref[:, cs] = _ffn_chunk(cs, hcat, prev, wg_ref, wu_ref, cw_ref, cb_ref, gs_ref, act_ref)
    o_ref[...] = x + _dot(act_ref[...], wd_ref[...])


def _resident(a):
    zeros = (0,) * a.ndim
    return pl.BlockSpec(a.shape, lambda *_: zeros, pipeline_mode=pl.Buffered(1))


def _ffn_weights(g, w_gate, w_up, conv_w, conv_b, w_down):
    return (g.reshape(1, -1), w_gate.astype(BF16), w_up.astype(BF16), conv_w, conv_b.reshape(1, -1),
            w_down.astype(BF16))


def _ffn_prompt(x, parts, w_out, g, w_gate, w_up, conv_w, conv_b, w_down, *, tm):
    B, T, D = x.shape
    dff = w_gate.shape[1]
    ws = _ffn_weights(g, w_gate, w_up, conv_w, conv_b, w_down)
    w_outs, off = [], 0
    for p in parts:
        w_outs.append(w_out[off:off + p.shape[2]].astype(BF16))
        off += p.shape[2]
    hpt = tm // HALO
    main = lambda w: pl.BlockSpec((1, tm, w), lambda b, t: (b, t, 0))
    halo = lambda w: pl.BlockSpec((1, HALO, w), lambda b, t: (b, jnp.maximum(t * hpt - 1, 0), 0))
    return pl.pallas_call(
        functools.partial(_ffn_prompt_kernel, len(parts)), grid=(B, T // tm),
        in_specs=[main(D), halo(D)] + [main(p.shape[2]) for p in parts] + [halo(p.shape[2]) for p in parts]
                 + [_resident(w) for w in w_outs] + [_resident(w) for w in ws],
        out_specs=[pl.BlockSpec((1, tm, D), lambda b, t: (b, t, 0)),
                   pl.BlockSpec((1, CONV_W - 1, dff), lambda b, t: (b, 0, 0))],
        out_shape=[jax.ShapeDtypeStruct((B, T, D), F32), jax.ShapeDtypeStruct((B, CONV_W - 1, dff), F32)],
        scratch_shapes=[pltpu.VMEM((tm + HALO, FF_CHUNK), F32), pltpu.VMEM((tm, dff), BF16),
                        pltpu.VMEM((tm + HALO, D), BF16)],
        compiler_params=_cparams("parallel", "arbitrary"), name="ffn_prompt",
    )(x, x, *parts, *parts, *w_outs, *ws)


def _ffn_sample(x, state, g, w_gate, w_up, conv_w, conv_b, w_down):
    B, T, D = x.shape
    dff = w_gate.shape[1]
    ws = _ffn_weights(g, w_gate, w_up, conv_w, conv_b, w_down)
    zeros = jnp.zeros((B, T, dff), F32)
    p1 = zeros.at[:, 0].set(state[:, 1]).reshape(B * T, dff)
    p2 = zeros.at[:, 0].set(state[:, 0]).at[:, 1].set(state[:, 1]).reshape(B * T, dff)
    args = (x.reshape(B * T, D), p1, p2) + ws
    y, gate = pl.pallas_call(
        functools.partial(_ffn_sample_kernel, T), grid=(1,),
        in_specs=[_resident(a) for a in args],
        out_specs=[pl.BlockSpec((B * T, D), lambda i: (0, 0)), pl.BlockSpec((B * T, dff), lambda i: (0, 0))],
        out_shape=[jax.ShapeDtypeStruct((B * T, D), F32), jax.ShapeDtypeStruct((B * T, dff), F32)],
        scratch_shapes=[pltpu.VMEM((B * T + HALO, FF_CHUNK), F32), pltpu.VMEM((B * T, dff), BF16)],
        compiler_params=_cparams("arbitrary"), name="ffn_sample",
    )(*args)
    return y.reshape(B, T, D), gate.reshape(B, T, dff)[:, T - (CONV_W - 1):]


def _proj_diff_kernel(x_ref, g_ref, w_ref, qn_ref, kn_ref, gmat_ref, q_ref, k_ref, v_ref, kb_ref, vb_ref):
    wq = q_ref.shape[2]
    hb = _rmsnorm(x_ref[0], g_ref[...]).astype(BF16)
    gmat = gmat_ref[...]
    q_ref[0] = (_head_rmsnorm(_dot(hb, w_ref[:, :wq]), gmat, qn_ref[...]) * QK_SCALE).astype(BF16)
    k = _head_rmsnorm(_dot(hb, w_ref[:, wq:2 * wq]), gmat, kn_ref[...])
    k_ref[0] = k
    kb_ref[0] = k.astype(BF16)
    v = _dot(hb, w_ref[:, 2 * wq:])
    v_ref[0] = v
    vb_ref[0] = v.astype(BF16)


def _proj_diff(x, g, w_in, qn, kn, *, tm):
    B, T, D = x.shape
    wv = w_in.shape[1] // 3
    wq = wv
    reps = wq // HEAD_DIM
    args = (x, g.reshape(1, D), w_in.astype(BF16), jnp.tile(qn, reps).reshape(1, wq),
            jnp.tile(kn, reps).reshape(1, wq), _head_mean_matrix())
    full = lambda a: pl.BlockSpec(a.shape, lambda b, t: (0,) * a.ndim)
    row = lambda w: pl.BlockSpec((1, tm, w), lambda b, t: (b, t, 0))
    sds = lambda w, dt: jax.ShapeDtypeStruct((B, T, w), dt)
    return pl.pallas_call(
        _proj_diff_kernel, grid=(B, T // tm),
        in_specs=[row(D)] + [full(a) for a in args[1:]],
        out_specs=[row(wq), row(wq), row(wv), row(wq), row(wv)],
        out_shape=[sds(wq, BF16), sds(wq, F32), sds(wv, F32), sds(wq, BF16), sds(wv, BF16)],
        compiler_params=_cparams("parallel", "parallel"), name="proj_diff",
    )(*args)


def _t5_bucket(rel):
    rel = np.asarray(rel)
    max_exact = N_BUCKETS // 2
    relf = np.maximum(rel, 1).astype(np.float32)
    large = max_exact + (np.log(relf / np.float32(max_exact)) / np.float32(math.log(MAX_DISTANCE / max_exact))
                         * np.float32(N_BUCKETS - max_exact)).astype(np.int32)
    large = np.minimum(large, N_BUCKETS - 1)
    return np.where(rel < max_exact, rel, large).astype(np.int32)


def _near_buckets(q_pos):
    r = np.asarray(q_pos)[:, None]
    c = np.arange(BLK)[None, :]
    assert _t5_bucket(2 * BLK - (BLK - 1)) == N_BUCKETS - 1
    return jnp.asarray(np.stack([_t5_bucket(np.maximum(r - c, 0)), _t5_bucket(BLK + r - c)]))


def _diff_lambda(lq1_ref, lk1_ref, lq2_ref, lk2_ref, lam_init):
    s1 = jnp.sum(lq1_ref[...] * lk1_ref[...], axis=1, keepdims=True)
    s2 = jnp.sum(lq2_ref[...] * lk2_ref[...], axis=1, keepdims=True)
    return jnp.exp(s1) - jnp.exp(s2) + lam_init


def _diff_prompt_kernel(lam_init, rb_ref, q_ref, k_ref, v_ref, bk_ref, lq1_ref, lk1_ref, lq2_ref, lk2_ref,
                        sub_ref, o_ref, m_ref, l_ref, acc_ref, bias_ref, kbd_ref, vbd_ref):
    T = q_ref.shape[1]
    units, _, tb, _ = bias_ref.shape
    nsub = tb // BLK
    lo16 = _head0_mask(BF16, tb)
    lam = _diff_lambda(lq1_ref, lk1_ref, lq2_ref, lk2_ref, lam_init)
    cols = [slice(u * LANES, (u + 1) * LANES) for u in range(units)]
    heads = [pl.program_id(1) * units + u for u in range(units)]
    far_bias = [rb_ref[N_BUCKETS - 1, h] for h in heads]

    for u in range(units):
        near = []
        for d in range(2):
            tile_bias = jnp.zeros((BLK, BLK), F32)
            for b in range(N_BUCKETS):
                tile_bias = jnp.where(bk_ref[d] == b, rb_ref[b, heads[u]], tile_bias)
            near.append(tile_bias)
        bias_ref[u] = jnp.full(bias_ref.shape[1:], far_bias[u], F32)
        for a in range(nsub):
            bias_ref[u, 0, a * BLK:(a + 1) * BLK, a * BLK:(a + 1) * BLK] = near[0]
            if a >= 1:
                bias_ref[u, 0, a * BLK:(a + 1) * BLK, (a - 1) * BLK:a * BLK] = near[1]
        bias_ref[u, 1, 0:BLK, (nsub - 1) * BLK:nsub * BLK] = near[1]

    def prepare(j, carry):
        ks = pl.ds(pl.multiple_of(j * tb, tb), tb)
        for u in range(units):
            kbd_ref[u, j] = _block_diag_pair(k_ref[0, ks, cols[u]], lo16)
            vb = v_ref[0, ks, cols[u]]
            zero = jnp.zeros_like(vb)
            vbd_ref[u, j] = jnp.concatenate([jnp.concatenate([vb, zero], axis=1),
                                             jnp.concatenate([zero, vb], axis=1)], axis=0)
        return carry

    lax.fori_loop(0, T // tb, prepare, 0)

    def tile(i, j, near, masked):
        qs = pl.ds(pl.multiple_of(i * tb, tb), tb)
        for u in range(units):
            z = _dot_nt(q_ref[0, qs, cols[u]], kbd_ref[u, j])
            ps, alphas = [], []
            for mp in range(2):
                s = z[:, mp * tb:(mp + 1) * tb]
                if near is not None:
                    s = s + bias_ref[u, near]
                if masked:
                    s = jnp.where(_causal_mask(tb), s, NEG)
                p, alpha = _softmax_step(s, m_ref, l_ref, 2 * u + mp, None if near is not None else far_bias[u])
                ps.append(p.astype(BF16))
                alphas.append(alpha)
            pv = _dot(jnp.concatenate(ps, axis=1), vbd_ref[u, j])
            acc_ref[u] = jnp.concatenate(alphas, axis=1) * acc_ref[u] + pv

    def qblock(i, carry):
        qs = pl.ds(pl.multiple_of(i * tb, tb), tb)
        m_ref[...] = jnp.full_like(m_ref, NEG)
        l_ref[...] = jnp.zeros_like(l_ref)
        acc_ref[...] = jnp.zeros_like(acc_ref)
        tile(i, i, 0, True)

        @pl.when(i >= 1)
        def _():
            tile(i, i - 1, 1, False)

        _paired_loop(jnp.maximum(i - 1, 0), lambda j: tile(i, j, None, False))
        for u in range(units):
            o_map = [acc_ref[u, :, mp * LANES:(mp + 1) * LANES] / jnp.sum(l_ref[2 * u + mp], axis=1, keepdims=True)
                     for mp in range(2)]
            o = o_map[0] - lam * o_map[1]
            o_ref[0, qs, cols[u]] = (_rmsnorm(o, sub_ref[...]) * (1.0 - lam_init)).astype(o_ref.dtype)
        return carry

    lax.fori_loop(0, T // tb, qblock, 0)


def _diff_prompt(q, kb, vb, rel_bias, lq1, lk1, lq2, lk2, subln, lam_init, *, tb, units):
    B, T, W = q.shape
    n_heads = W // LANES
    assert n_heads % units == 0
    buckets = _near_buckets(np.arange(BLK))
    spec = pl.BlockSpec((1, T, units * LANES), lambda b, h: (b, 0, h))
    small = lambda a: pl.BlockSpec(a.shape, lambda b, h: (0,) * a.ndim)
    vecs = [a.reshape(1, -1) for a in (lq1, lk1, lq2, lk2, subln)]
    nb = T // tb
    return pl.pallas_call(
        functools.partial(_diff_prompt_kernel, lam_init), grid=(B, n_heads // units),
        in_specs=[pl.BlockSpec(memory_space=pltpu.SMEM)] + [spec] * 3 + [small(buckets)] + [small(a) for a in vecs],
        out_specs=spec,
        scratch_shapes=[pltpu.VMEM((2 * units, tb, LANES), F32), pltpu.VMEM((2 * units, tb, LANES), F32),
                        pltpu.VMEM((units, tb, 2 * LANES), F32), pltpu.VMEM((units, 2, tb, tb), F32),
                        pltpu.VMEM((units, nb, 2 * tb, LANES), BF16),
                        pltpu.VMEM((units, nb, 2 * tb, 2 * LANES), BF16)],
        out_shape=jax.ShapeDtypeStruct((B, T, W), BF16),
        compiler_params=_cparams("parallel", "parallel"), name="diff_prompt",
    )(rel_bias, q, kb, vb, buckets, *vecs)


GROUPS_PER_CHUNK = MXU_DIM // HEAD_DIM


def _chunked_queries(q):
    B, T, W = q.shape
    qg = q.reshape(B, T, W // MXU_DIM, GROUPS_PER_CHUNK, HEAD_DIM).transpose(0, 2, 3, 1, 4)
    eye = jnp.eye(GROUPS_PER_CHUNK, dtype=q.dtype)
    out = qg[:, :, :, :, None, :] * eye[None, None, :, None, :, None]
    return out.reshape(B, W // MXU_DIM, GROUPS_PER_CHUNK * T, MXU_DIM)


def _keys_on_lanes(cache):
    nd = cache.ndim
    t = cache.transpose((0, 1) + tuple(range(3, nd)) + (2,))
    return t.reshape(cache.shape[0], cache.shape[1], -1, cache.shape[2])


def _scores(qc_ref, key_tiles, transposed):
    rows = []
    for c in range(qc_ref.shape[1]):
        cs = slice(c * MXU_DIM, (c + 1) * MXU_DIM)
        if transposed:
            rows.append(_dot(qc_ref[0, c], jnp.concatenate([kt[cs, :].astype(BF16) for kt in key_tiles], axis=1)))
        else:
            rows.append(_dot_nt(qc_ref[0, c], jnp.concatenate([kt[:, cs].astype(BF16) for kt in key_tiles], axis=0)))
    return jnp.concatenate(rows, axis=0)


def _mix_sample_kernel(seq, n_sb, n_pages, pt_ref, qc_ref, kn_ref, vn_ref, lfn_ref, *refs):
    kp = refs[:n_pages]
    vp = refs[n_pages:2 * n_pages]
    lfp = refs[2 * n_pages:3 * n_pages]
    o_ref, run_ref, m_ref, l_ref, acc_ref = refs[3 * n_pages:]
    j = pl.program_id(1)
    sb = n_sb * seq
    n_fox = (BLK - sb) // seq
    rpc = GROUPS_PER_CHUNK * seq
    n_chunks = qc_ref.shape[1]
    suffix = _suffix_matrix()

    def block(z, lfts, pv_of, mask):
        n = len(lfts)
        zs, zf = z[:sb], z[sb:]
        ls = _log_sigmoid_scores(zs)
        l1m = ls - zs
        a_parts, s_parts = [], []
        for g in range(n):
            gs = slice(g * BLK, (g + 1) * BLK)
            elf = jnp.concatenate([jnp.broadcast_to(lfts[g][h:h + 1, :], (seq, BLK)) for h in range(n_fox)], axis=0)
            x = jnp.concatenate([l1m[:, gs], elf], axis=0)
            if mask is not None:
                x = jnp.where(mask, x, 0.0)
            cs = _dot(_split(x), suffix)
            run = run_ref[...]
            run_ref[...] = run + cs[:, BLK:]
            later = cs[:, :BLK] + run
            a = jnp.exp(ls[:, gs] + later[:sb])
            s = zf[:, gs] + later[sb:]
            if mask is not None:
                a = jnp.where(mask[:sb], a, 0.0)
                s = jnp.where(mask[sb:], s, NEG)
            a_parts.append(a)
            s_parts.append(s)
        p, alpha = _softmax_step(jnp.concatenate(s_parts, axis=1), m_ref, l_ref, 0)
        w = jnp.concatenate([jnp.concatenate(a_parts, axis=1), p], axis=0).astype(BF16)
        for c in range(n_chunks):
            pv = pv_of(c, w[c * rpc:(c + 1) * rpc])
            if (c + 1) * rpc <= sb:
                acc_ref[c] += pv
            else:
                al = alpha[c * rpc - sb:(c + 1) * rpc - sb]
                acc_ref[c] = jnp.concatenate([al] * (MXU_DIM // LANES), axis=1) * acc_ref[c] + pv

    @pl.when(j == 0)
    def _():
        run_ref[...] = jnp.zeros_like(run_ref)
        m_ref[...] = jnp.full_like(m_ref, NEG)
        l_ref[...] = jnp.zeros_like(l_ref)
        acc_ref[...] = jnp.zeros_like(acc_ref)
        lane = lax.broadcasted_iota(jnp.int32, (BLK, BLK), 1)
        row = lax.broadcasted_iota(jnp.int32, (BLK, BLK), 0)
        own = lane < row % seq + jnp.where(row < sb, 0, 1)
        kn = _pad_rows(kn_ref[0], BLK)
        vn = _pad_rows(vn_ref[0], BLK).astype(BF16)
        block(_scores(qc_ref, [kn], False), [lfn_ref[0]],
              lambda c, w: _dot(w, vn[:, c * MXU_DIM:(c + 1) * MXU_DIM]), own)

    def page_values(c, w):
        cs = slice(c * MXU_DIM, (c + 1) * MXU_DIM)
        return _dot_nt(w, jnp.concatenate([v[cs, :].astype(BF16) for v in vp], axis=1))

    block(_scores(qc_ref, kp, True), [r[...] for r in lfp], page_values, None)

    @pl.when(j == pl.num_programs(1) - 1)
    def _():
        col_group = lax.broadcasted_iota(jnp.int32, (seq, MXU_DIM), 1) // HEAD_DIM
        denom = jnp.sum(l_ref[0], axis=1, keepdims=True)
        outs = []
        for c in range(n_chunks):
            rows = acc_ref[c]
            if (c + 1) * rpc > sb:
                rows = rows / denom[c * rpc - sb:(c + 1) * rpc - sb]
            out = jnp.zeros((seq, MXU_DIM), F32)
            for g in range(GROUPS_PER_CHUNK):
                out = jnp.where(col_group == g, rows[g * seq:(g + 1) * seq, :], out)
            outs.append(out)
        o_ref[0] = jnp.concatenate(outs, axis=1)


def _paged_specs(shape_tail, layer, n_total, per_step):
    def spec(g):
        zeros = (0,) * len(shape_tail)
        return pl.BlockSpec((None, None) + shape_tail,
                            lambda b, j, pt: (layer, pt[b, n_total - 1 - (j * per_step + g)]) + zeros)
    return [spec(g) for g in range(per_step)]


def _mix_sample(q, k_new, v_new, logf_new, cache_k, cache_v, cache_logf, layer, page_table, n_sb):
    B, T, W = q.shape
    n_heads = W // HEAD_DIM
    n_fox = n_heads - n_sb
    n_total = page_table.shape[1]
    g = min(PAGES_PER_STEP, n_total)
    assert n_heads * T == BLK and cache_k.shape[2] == BLK and n_total % g == 0
    assert (n_sb * T) % (GROUPS_PER_CHUNK * T) == 0
    lfn = jnp.zeros((B, n_fox, BLK), F32).at[:, :, :T].set(logf_new.transpose(0, 2, 1))
    args = (_chunked_queries(q), k_new, v_new, lfn)
    per_seq = lambda a: pl.BlockSpec((1,) + a.shape[1:], lambda b, j, pt: (b,) + (0,) * (a.ndim - 1))
    return pl.pallas_call(
        functools.partial(_mix_sample_kernel, T, n_sb, g),
        grid_spec=pltpu.PrefetchScalarGridSpec(
            num_scalar_prefetch=1, grid=(B, n_total // g),
            in_specs=[per_seq(a) for a in args] + _paged_specs((W, BLK), layer, n_total, g) * 2
                     + _paged_specs((n_fox, BLK), layer, n_total, g),
            out_specs=pl.BlockSpec((1, T, W), lambda b, j, pt: (b, 0, 0)),
            scratch_shapes=[pltpu.VMEM((BLK, BLK), F32), pltpu.VMEM((1, n_fox * T, BLK), F32),
                            pltpu.VMEM((1, n_fox * T, BLK), F32),
                            pltpu.VMEM((W // MXU_DIM, GROUPS_PER_CHUNK * T, MXU_DIM), F32)]),
        out_shape=jax.ShapeDtypeStruct((B, T, W), F32),
        compiler_params=_cparams("parallel", "arbitrary"), name="mix_sample",
    )(page_table, *args, *([_keys_on_lanes(cache_k)] * g), *([_keys_on_lanes(cache_v)] * g),
      *([_keys_on_lanes(cache_logf)] * g))


def _diff_sample_kernel(seq, lam_init, n_pages, pt_ref, qc_ref, kn_ref, vn_ref, bk_ref, rbr_ref,
                        lq1_ref, lk1_ref, lq2_ref, lk2_ref, sub_ref, *refs):
    kp = refs[:n_pages]
    vp = refs[n_pages:2 * n_pages]
    o_ref, m_ref, l_ref, acc_ref, bias_ref = refs[2 * n_pages:]
    j = pl.program_id(1)
    n_heads = vn_ref.shape[2] // BLK
    rph = 2 * seq

    def near_bias(d):
        out = jnp.zeros((BLK, BLK), F32)
        for b in range(N_BUCKETS):
            out = jnp.where(bk_ref[d] == b, rbr_ref[:, b:b + 1], out)
        return out

    def block(s, values_of, mask):
        if mask is not None:
            s = jnp.where(mask, s, NEG)
        p, alpha = _softmax_step(s, m_ref, l_ref, 0)
        w = p.astype(BF16)
        pv = jnp.concatenate([_dot(w[h * rph:(h + 1) * rph], values_of(h)) for h in range(n_heads)], axis=0)
        acc_ref[...] = alpha * acc_ref[...] + pv

    far = jnp.broadcast_to(rbr_ref[:, N_BUCKETS - 1:N_BUCKETS], (BLK, BLK))

    @pl.when(j == 0)
    def _():
        m_ref[...] = jnp.full_like(m_ref, NEG)
        l_ref[...] = jnp.zeros_like(l_ref)
        acc_ref[...] = jnp.zeros_like(acc_ref)
        lane = lax.broadcasted_iota(jnp.int32, (BLK, BLK), 1)
        row = lax.broadcasted_iota(jnp.int32, (BLK, BLK), 0)
        kn = _pad_rows(kn_ref[0], BLK)
        block(_scores(qc_ref, [kn], False) + near_bias(0),
              lambda h: _pad_rows(vn_ref[0, :, h * BLK:(h + 1) * BLK], BLK).astype(BF16), lane <= row % seq)
        bias_ref[...] = jnp.concatenate([near_bias(1)] + [far] * (n_pages - 1), axis=1)

    @pl.when(j == 1)
    def _():
        bias_ref[...] = jnp.concatenate([far] * n_pages, axis=1)

    block(_scores(qc_ref, kp, True) + bias_ref[...],
          lambda h: jnp.concatenate([v[pl.ds(h, BLK, stride=n_heads), :].astype(BF16) for v in vp], axis=0), None)

    @pl.when(j == pl.num_programs(1) - 1)
    def _():
        rows = acc_ref[...] / jnp.sum(l_ref[0], axis=1, keepdims=True)
        lam = _diff_lambda(lq1_ref, lk1_ref, lq2_ref, lk2_ref, lam_init)
        sub = sub_ref[...]
        outs = []
        for h in range(n_heads):
            o = rows[h * rph:h * rph + seq] - lam * rows[h * rph + seq:(h + 1) * rph]
            outs.append(_rmsnorm(o, sub))
        o_ref[0] = jnp.concatenate(outs, axis=1) * (1.0 - lam_init)


def _diff_sample(q, k_new, v_new, cache_k, cache_v, layer, page_table, rel_bias, lq1, lk1, lq2, lk2, subln, lam_init):
    B, T, W = q.shape
    n_heads = cache_v.shape[3]
    n_total = page_table.shape[1]
    g = min(PAGES_PER_STEP, n_total)
    assert (W // HEAD_DIM) * T == BLK and cache_k.shape[2] == BLK and n_total % g == 0
    assert cache_v.shape[4] == BLK and v_new.shape[2] == n_heads * BLK
    L, P = cache_v.shape[:2]
    v_rows = cache_v.reshape(L, P, BLK * n_heads, BLK)
    buckets = _near_buckets(np.arange(BLK) % T)
    rb_rows = jnp.repeat(rel_bias.T, 2 * T, axis=0)
    vecs = [a.reshape(1, -1) for a in (lq1, lk1, lq2, lk2, subln)]
    args = (_chunked_queries(q), k_new, v_new)
    per_seq = lambda a: pl.BlockSpec((1,) + a.shape[1:], lambda b, j, pt: (b,) + (0,) * (a.ndim - 1))
    small = lambda a: pl.BlockSpec(a.shape, lambda b, j, pt: (0,) * a.ndim)
    return pl.pallas_call(
        functools.partial(_diff_sample_kernel, T, lam_init, g),
        grid_spec=pltpu.PrefetchScalarGridSpec(
            num_scalar_prefetch=1, grid=(B, n_total // g),
            in_specs=[per_seq(a) for a in args] + [small(buckets), small(rb_rows)] + [small(a) for a in vecs]
                     + _paged_specs((W, BLK), layer, n_total, g) + _paged_specs((BLK * n_heads, BLK), layer, n_total, g),
            out_specs=pl.BlockSpec((1, T, n_heads * BLK), lambda b, j, pt: (b, 0, 0)),
            scratch_shapes=[pltpu.VMEM((1, BLK, BLK), F32), pltpu.VMEM((1, BLK, BLK), F32),
                            pltpu.VMEM((BLK, BLK), F32), pltpu.VMEM((BLK, g * BLK), F32)]),
        out_shape=jax.ShapeDtypeStruct((B, T, n_heads * BLK), F32),
        compiler_params=_cparams("parallel", "arbitrary"), name="diff_sample",
    )(page_table, *args, buckets, rb_rows, *vecs, *([_keys_on_lanes(cache_k)] * g), *([v_rows] * g))


def kernel(x_prompt, x_sample, cache_k_mix, cache_v_mix, cache_logf_mix, cache_k_diff, cache_v_diff, state_ffn_conv, page_table, rel_bias, norm_mix, w_in_mix, b_forget, qnorm_fox, knorm_fox, w_out_mix, norm_diff, w_in_diff, qnorm_diff, knorm_diff, lambda_q1, lambda_k1, lambda_q2, lambda_k2, subln_diff, w_out_diff, norm_ffn, w_gate, w_up, conv_w, conv_b, w_down):
    B, T, D = x_prompt.shape
    DB, DT, _ = x_sample.shape
    depth = norm_ffn.shape[0]
    n_fox = b_forget.shape[1]
    n_mix = cache_k_mix.shape[3]
    n_sb = n_mix - n_fox
    n_diff = cache_k_diff.shape[3]
    sb_pairs = n_sb * HEAD_DIM // LANES
    fox_pairs = n_fox * HEAD_DIM // LANES
    tm = min(T, PROMPT_ROW_TILE)
    tb = min(T, ATT_BLOCK)
    xp, xs = x_prompt, x_sample
    outs = {n: [] for n in ("kmp", "vmp", "lfp", "kms", "vms", "lfs", "kdp", "vdp", "kds", "vds", "cp", "cs")}
    for l in range(depth):
        j = l // 2
        if l % 2 == 0:
            pw = (norm_mix[j], w_in_mix[j], b_forget[j], qnorm_fox[j], knorm_fox[j])
            q, k, v, kb, vb, logf, c, ct = _proj_mix(xp, *pw, tm=tm, with_cumsum=True)
            o_sb = _sb_prompt(q, kb, vb, sb_pairs, group=min(T // BLK, SB_GROUP))
            o_fx = _fox_prompt(q, kb, vb, c, ct, sb_pairs, fox_pairs, tb=tb, units=ATT_UNITS)
            mixed, w_out = [o_sb, o_fx], w_out_mix[j]
            outs["kmp"].append(k.reshape(B, T, n_mix, HEAD_DIM))
            outs["vmp"].append(v.reshape(B, T, n_mix, HEAD_DIM))
            outs["lfp"].append(logf)
            qs, ks, vs, _, _, lfs = _proj_mix(xs.reshape(1, DB * DT, D), *pw, tm=DB * DT, with_cumsum=False)
            ks, vs, lfs = ks.reshape(DB, DT, -1), vs.reshape(DB, DT, -1), lfs.reshape(DB, DT, n_fox)
            o = _mix_sample(qs.reshape(DB, DT, -1), ks, vs, lfs, cache_k_mix, cache_v_mix, cache_logf_mix, j,
                            page_table, n_sb)
            xs = _out_proj(xs.reshape(DB * DT, D), [o.reshape(DB * DT, -1).astype(BF16)], w_out_mix[j],
                           tm=DB * DT).reshape(DB, DT, D)
            outs["kms"].append(ks.reshape(DB, DT, n_mix, HEAD_DIM))
            outs["vms"].append(vs.reshape(DB, DT, n_mix, HEAD_DIM))
            outs["lfs"].append(lfs)
        else:
            lam_init = 0.8 - 0.6 * math.exp(-0.3 * l)
            pw = (norm_diff[j], w_in_diff[j], qnorm_diff[j], knorm_diff[j])
            lam_w = (lambda_q1[j], lambda_k1[j], lambda_q2[j], lambda_k2[j], subln_diff[j], lam_init)
            q, k, v, kb, vb = _proj_diff(xp, *pw, tm=tm)
            o = _diff_prompt(q, kb, vb, rel_bias, *lam_w, tb=tb, units=ATT_UNITS)
            mixed, w_out = [o], w_out_diff[j]
            outs["kdp"].append(k.reshape(B, T, n_diff, 2, HEAD_DIM))
            outs["vdp"].append(v.reshape(B, T, n_diff, 2 * HEAD_DIM))
            qs, ks, vs, _, _ = _proj_diff(xs.reshape(1, DB * DT, D), *pw, tm=DB * DT)
            ks, vs = ks.reshape(DB, DT, -1), vs.reshape(DB, DT, -1)
            o = _diff_sample(qs.reshape(DB, DT, -1), ks, vs, cache_k_diff, cache_v_diff, j, page_table, rel_bias,
                             *lam_w)
            xs = _out_proj(xs.reshape(DB * DT, D), [o.reshape(DB * DT, -1).astype(BF16)], w_out_diff[j],
                           tm=DB * DT).reshape(DB, DT, D)
            outs["kds"].append(ks.reshape(DB, DT, n_diff, 2, HEAD_DIM))
            outs["vds"].append(vs.reshape(DB, DT, n_diff, 2 * HEAD_DIM))
        fw = (norm_ffn[l], w_gate[l], w_up[l], conv_w[l], conv_b[l], w_down[l])
        xp, cp = _ffn_prompt(xp, mixed, w_out, *fw, tm=min(T, FFN_ROW_TILE))
        xs, cs = _ffn_sample(xs, state_ffn_conv[l], *fw)
        outs["cp"].append(cp)
        outs["cs"].append(cs)
    st = {n: jnp.stack(a) for n, a in outs.items()}
    return (xp, xs, st["kmp"], st["vmp"], st["lfp"], st["kms"], st["vms"], st["lfs"], st["kdp"], st["vdp"],
            st["kds"], st["vds"], st["cp"], st["cs"])
```
